```python
import jax, jax.numpy as jnp
from jax import lax
import numpy as np


D_MODEL = 2048
BATCH = 16
SEQ = 2048
DEPTH = 2

CHUNK = 64
EPS = 1e-6
A_BLOCK = 128
A_GROUPS = 4
A_WIDTH = D_MODEL // 2
A_GROUP_DIM = A_WIDTH // A_GROUPS
B_WINDOWS = (2, 4, 8, 16)
B_GROUPS = len(B_WINDOWS)
B_WIDTH = D_MODEL // 2
B_GROUP_DIM = B_WIDTH // B_GROUPS
EVEN_IN = 2 * A_WIDTH + B_WIDTH
EVEN_MIX = A_WIDTH + B_WIDTH
C_HEADS = 16
C_HEAD_DIM = 128
C_WIDTH = C_HEADS * C_HEAD_DIM
Q_BLOCK = 128
ODD_IN = 3 * C_WIDTH + C_HEADS
D_FF = 5632
CONV_WIDTH = 3
N_EVEN = (DEPTH + 1) // 2
N_ODD = DEPTH // 2

kernel_name = 'hybrid_gmlp_pool_fox_encoder'


def rms_norm(x, g):
    x32 = x.astype(jnp.float32)
    y = x32 * lax.rsqrt(jnp.mean(x32 * x32, axis=-1, keepdims=True) + EPS)
    return (y * g.astype(jnp.float32)).astype(x.dtype)


def group_layer_norm(v, g, b):
    v32 = v.astype(jnp.float32)
    mu = jnp.mean(v32, axis=-1, keepdims=True)
    var = jnp.mean(jnp.square(v32 - mu), axis=-1, keepdims=True)
    y = (v32 - mu) * lax.rsqrt(var + EPS) * g.astype(jnp.float32) + b.astype(jnp.float32)
    return y.astype(v.dtype)


def block_causal_mask(n):
    c = jnp.arange(n) // CHUNK
    return c[:, None] >= c[None, :]


def spatial_gating(z, ln_g, ln_b, w_s, b_s):
    bsz, s, _ = z.shape
    u = z[..., :A_WIDTH]
    v = z[..., A_WIDTH:].reshape(bsz, s, A_GROUPS, A_GROUP_DIM)
    v = group_layer_norm(v, ln_g, ln_b)
    v = v.reshape(bsz, s // A_BLOCK, A_BLOCK, A_GROUPS, A_GROUP_DIM)
    w = jnp.where(block_causal_mask(A_BLOCK)[None], w_s, jnp.zeros((), w_s.dtype))
    gate = jnp.einsum('gts,bnsgd->bntgd', w, v) + b_s.T[None, None, :, :, None]
    return u * gate.reshape(bsz, s, A_WIDTH)


def multiscale_pool(p, w_pool, scale):
    bsz, s, _ = p.shape
    p = p.reshape(bsz, s, B_GROUPS, B_GROUP_DIM)
    pos = jnp.arange(s)
    outs = []
    for j, w in enumerate(B_WINDOWS):
        xg = p[:, :, j, :]
        x32 = xg.astype(jnp.float32)
        cs = jnp.cumsum(x32, axis=1)
        lagged = jnp.pad(cs, ((0, 0), (w, 0), (0, 0)))[:, :s]
        count = jnp.minimum(pos + 1, w).astype(jnp.float32)[None, :, None]
        d = ((cs - lagged) / count - x32).astype(p.dtype)
        outs.append(d @ w_pool[j])
    return jnp.concatenate(outs, axis=-1) * scale


def forgetting_attention(q, k, v, log_f):
    bsz, s, h, dh = q.shape
    c = jnp.cumsum(log_f, axis=1).transpose(0, 2, 1)
    scale = dh ** -0.5
    outs = []
    for i in range(s // Q_BLOCK):
        q0 = i * Q_BLOCK
        kl = q0 + Q_BLOCK
        logits = jnp.einsum('bqhd,bkhd->bhqk', q[:, q0:kl], k[:, :kl]).astype(jnp.float32) * scale
        logits = logits + c[:, :, q0:kl, None] - c[:, :, None, :kl]
        causal = (q0 + jnp.arange(Q_BLOCK))[:, None] >= jnp.arange(kl)[None, :]
        logits = jnp.where(causal, logits, -jnp.inf)
        probs = jax.nn.softmax(logits, axis=-1).astype(v.dtype)
        outs.append(jnp.einsum('bhqk,bkhd->bqhd', probs, v[:, :kl]))
    return jnp.concatenate(outs, axis=1)


def conv_ffn(h, w_up, conv_w, conv_b, w_down):
    s = h.shape[1]
    up = h @ w_up
    padded = jnp.pad(up, ((0, 0), (CONV_WIDTH - 1, 0), (0, 0)))
    conv = conv_b
    for j in range(CONV_WIDTH):
        conv = conv + padded[:, j:j + s] * conv_w[j]
    gate, val = conv[..., :D_FF], conv[..., D_FF:]
    return (jax.nn.silu(gate) * val) @ w_down


def setup_inputs(seed: int = 0) -> dict:
    key = jax.random.key(seed)
    ks = jax.random.split(key, 20)
    f32 = jnp.float32

    def nrm(k, shape, scale):
        return jax.random.normal(k, shape, f32) * scale

    return {
        'x': nrm(ks[0], (BATCH, SEQ, D_MODEL), 1.0),
        'norm_mix_g': 1.0 + nrm(ks[1], (DEPTH, D_MODEL), 0.05),
        'norm_ffn_g': 1.0 + nrm(ks[2], (DEPTH, D_MODEL), 0.05),
        'final_norm_g': 1.0 + nrm(ks[3], (D_MODEL,), 0.05),
        'w_in_even': nrm(ks[4], (N_EVEN, D_MODEL, EVEN_IN), D_MODEL ** -0.5),
        'ln_v_g': 1.0 + nrm(ks[5], (N_EVEN, A_GROUPS, A_GROUP_DIM), 0.05),
        'ln_v_b': nrm(ks[6], (N_EVEN, A_GROUPS, A_GROUP_DIM), 0.02),
        'w_spatial': nrm(ks[7], (N_EVEN, A_GROUPS, A_BLOCK, A_BLOCK), A_BLOCK ** -0.5),
        'b_spatial': 1.0 + nrm(ks[8], (N_EVEN, A_GROUPS, A_BLOCK), 0.1),
        'w_pool': nrm(ks[9], (N_EVEN, B_GROUPS, B_GROUP_DIM, B_GROUP_DIM), B_GROUP_DIM ** -0.5),
        'pool_scale': 1.0 + nrm(ks[10], (N_EVEN, B_WIDTH), 0.1),
        'w_out_even': nrm(ks[11], (N_EVEN, EVEN_MIX, D_MODEL), EVEN_MIX ** -0.5),
        'w_in_odd': nrm(ks[12], (N_ODD, D_MODEL, ODD_IN), D_MODEL ** -0.5),
        'b_forget': jax.random.uniform(ks[13], (N_ODD, C_HEADS), f32, 1.0, 5.0),
        'w_out_odd': nrm(ks[14], (N_ODD, C_WIDTH, D_MODEL), C_WIDTH ** -0.5),
        'w_up': nrm(ks[15], (DEPTH, D_MODEL, 2 * D_FF), D_MODEL ** -0.5),
        'conv_w': nrm(ks[16], (DEPTH, CONV_WIDTH, 2 * D_FF), CONV_WIDTH ** -0.5),
        'conv_b': nrm(ks[17], (DEPTH, 2 * D_FF), 0.02),
        'w_down': nrm(ks[18], (DEPTH, D_FF, D_MODEL), D_FF ** -0.5),
    }


def reference(x, norm_mix_g, norm_ffn_g, final_norm_g, w_in_even, ln_v_g, ln_v_b,
              w_spatial, b_spatial, w_pool, pool_scale, w_out_even, w_in_odd,
              b_forget, w_out_odd, w_up, conv_w, conv_b, w_down):
    bsz, s, _ = x.shape
    for i in range(DEPTH):
        h = rms_norm(x, norm_mix_g[i])
        if i % 2 == 0:
            e = i // 2
            p = h @ w_in_even[e]
            z = jax.nn.gelu(p[..., :2 * A_WIDTH], approximate=False)
            y_a = spatial_gating(z, ln_v_g[e], ln_v_b[e], w_spatial[e], b_spatial[e])
            y_b = multiscale_pool(p[..., 2 * A_WIDTH:], w_pool[e], pool_scale[e])
            mix = jnp.concatenate([y_a, y_b], axis=-1) @ w_out_even[e]
        else:
            o = i // 2
            p = h @ w_in_odd[o]
            q = p[..., :C_WIDTH].reshape(bsz, s, C_HEADS, C_HEAD_DIM)
            k = p[..., C_WIDTH:2 * C_WIDTH].reshape(bsz, s, C_HEADS, C_HEAD_DIM)
            v = p[..., 2 * C_WIDTH:3 * C_WIDTH].reshape(bsz, s, C_HEADS, C_HEAD_DIM)
            f_logit = (p[..., 3 * C_WIDTH:] + b_forget[o]).astype(jnp.float32)
            log_f = jax.nn.log_sigmoid(f_logit)
            att = forgetting_attention(q, k, v, log_f).reshape(bsz, s, C_WIDTH)
            mix = att @ w_out_odd[o]
        x = x + mix
        x = x + conv_ffn(rms_norm(x, norm_ffn_g[i]), w_up[i], conv_w[i], conv_b[i], w_down[i])
    return rms_norm(x, final_norm_g)
```

```python
import functools

import jax
import jax.numpy as jnp
from jax import lax
from jax.experimental import pallas as pl
from jax.experimental.pallas import tpu as pltpu

D_MODEL = 2048
SEQ = 2048
EPS = 1e-6
CHUNK = 64
A_BLOCK = 128
A_GROUPS = 4
A_WIDTH = D_MODEL // 2
A_GROUP_DIM = A_WIDTH // A_GROUPS
B_WINDOWS = (2, 4, 8, 16)
B_WIDTH = D_MODEL // 2
B_GROUP_DIM = B_WIDTH // len(B_WINDOWS)
C_HEADS = 16
C_HEAD_DIM = 128
C_WIDTH = C_HEADS * C_HEAD_DIM
D_FF = 5632
CONV_WIDTH = 3

LANES = 128
SUBLANES = 8
VMEM_LIMIT_BYTES = 60 * 1024 * 1024

POOL_HALO = 16
CONV_HALO = SUBLANES

BF16 = jnp.bfloat16
F32 = jnp.float32


def _dot(a, b):
    return jnp.dot(a, b, preferred_element_type=F32)


def _rms_norm(x, g):
    return x * lax.rsqrt(jnp.mean(x * x, axis=-1, keepdims=True) + EPS) * g


def _gelu_exact(x):
    return 0.5 * x * (1.0 + lax.erf(x * (0.5 ** 0.5)))


def _resident(shape):
    zeros = (0,) * len(shape)
    return pl.BlockSpec(shape, lambda *_: zeros, pipeline_mode=pl.Buffered(1))


def _params(n_grid):
    return pltpu.CompilerParams(
        dimension_semantics=("arbitrary",) * n_grid,
        vmem_limit_bytes=VMEM_LIMIT_BYTES,
    )


def _even_mixer_kernel(x_ref, g_ref, win_ref, lng_ref, lnb_ref, ws_ref, bs_ref,
                       wpool_ref, pscale_ref, wout_ref, o_ref, pool_ext, mix_scr, *, tm):
    si = pl.program_id(1)
    x = x_ref[0]
    h = _rms_norm(x, g_ref[...]).astype(BF16)

    u = _gelu_exact(_dot(h, win_ref[:, 0:A_WIDTH]))
    v = _gelu_exact(_dot(h, win_ref[:, A_WIDTH:2 * A_WIDTH]))
    t_idx = lax.broadcasted_iota(jnp.int32, (A_BLOCK, A_BLOCK), 0) // CHUNK
    s_idx = lax.broadcasted_iota(jnp.int32, (A_BLOCK, A_BLOCK), 1) // CHUNK
    causal = t_idx >= s_idx
    for g in range(A_GROUPS):
        cols = slice(g * A_GROUP_DIM, (g + 1) * A_GROUP_DIM)
        vg = v[:, cols]
        mu = jnp.mean(vg, axis=-1, keepdims=True)
        var = jnp.mean(jnp.square(vg - mu), axis=-1, keepdims=True)
        vn = ((vg - mu) * lax.rsqrt(var + EPS) * lng_ref[:, cols] + lnb_ref[:, cols]).astype(BF16)
        w_g = jnp.where(causal, ws_ref[g], 0.0).astype(BF16)
        for n in range(tm // A_BLOCK):
            rows = slice(n * A_BLOCK, (n + 1) * A_BLOCK)
            gate = _dot(w_g, vn[rows]) + bs_ref[g]
            mix_scr[rows, cols] = (u[rows, cols] * gate).astype(BF16)

    @pl.when(si == 0)
    def _():
        pool_ext[0:POOL_HALO, :] = jnp.zeros((POOL_HALO, B_WIDTH), F32)

    @pl.when(si > 0)
    def _():
        pool_ext[0:POOL_HALO, :] = pool_ext[tm:tm + POOL_HALO, :]

    pool_ext[POOL_HALO:POOL_HALO + tm, :] = _dot(h, win_ref[:, 2 * A_WIDTH:2 * A_WIDTH + B_WIDTH])
    pos = si * tm + lax.broadcasted_iota(jnp.int32, (tm, 1), 0)
    for j, w in enumerate(B_WINDOWS):
        cols = slice(j * B_GROUP_DIM, (j + 1) * B_GROUP_DIM)
        xg = pool_ext[POOL_HALO:POOL_HALO + tm, cols]
        acc = xg
        for k in range(1, w):
            acc = acc + pool_ext[POOL_HALO - k:POOL_HALO - k + tm, cols]
        count = jnp.minimum(pos + 1, w).astype(F32)
        d = (acc / count - xg).astype(BF16)
        yb = _dot(d, wpool_ref[j]) * pscale_ref[:, cols]
        mix_scr[:, A_WIDTH + j * B_GROUP_DIM:A_WIDTH + (j + 1) * B_GROUP_DIM] = yb.astype(BF16)

    o_ref[0] = x + _dot(mix_scr[...], wout_ref[...])


def _even_mixer(x, g, w_in, ln_g, ln_b, w_s, b_s, w_pool, p_scale, w_out, *, tm=256):
    bsz = x.shape[0]
    even_in = 2 * A_WIDTH + B_WIDTH
    row_spec = pl.BlockSpec((1, tm, D_MODEL), lambda b, i: (b, i, 0))
    return pl.pallas_call(
        functools.partial(_even_mixer_kernel, tm=tm),
        grid=(bsz, SEQ // tm),
        in_specs=[
            row_spec,
            _resident((1, D_MODEL)),
            _resident((D_MODEL, even_in)),
            _resident((1, A_WIDTH)),
            _resident((1, A_WIDTH)),
            _resident((A_GROUPS, A_BLOCK, A_BLOCK)),
            _resident((A_GROUPS, A_BLOCK, 1)),
            _resident((len(B_WINDOWS), B_GROUP_DIM, B_GROUP_DIM)),
            _resident((1, B_WIDTH)),
            _resident((A_WIDTH + B_WIDTH, D_MODEL)),
        ],
        out_specs=row_spec,
        out_shape=jax.ShapeDtypeStruct(x.shape, F32),
        scratch_shapes=[
            pltpu.VMEM((tm + POOL_HALO, B_WIDTH), F32),
            pltpu.VMEM((tm, A_WIDTH + B_WIDTH), BF16),
        ],
        compiler_params=_params(2),
        name="even_mixer",
    )(x, g.reshape(1, D_MODEL), w_in.astype(BF16), ln_g.reshape(1, A_WIDTH),
      ln_b.reshape(1, A_WIDTH), w_s, b_s.reshape(A_GROUPS, A_BLOCK, 1),
      w_pool.astype(BF16), p_scale.reshape(1, B_WIDTH), w_out.astype(BF16))


def _conv_ffn_kernel(x_ref, g_ref, wg_ref, wv_ref, cwg_ref, cwv_ref, cbg_ref, cbv_ref,
                     wd_ref, gfin_ref, o_ref, h_scr, ext_g, ext_v, carry, *, tm, final_norm):
    si = pl.program_id(1)
    f = pl.program_id(2)

    @pl.when(f == 0)
    def _():
        x = x_ref[0]
        h_scr[...] = _rms_norm(x, g_ref[...]).astype(BF16)
        o_ref[0] = x

    h = h_scr[...]
    up_g = _dot(h, wg_ref[...])
    up_v = _dot(h, wv_ref[...])

    @pl.when(si == 0)
    def _():
        ext_g[0:CONV_HALO, :] = jnp.zeros((CONV_HALO, ext_g.shape[1]), F32)
        ext_v[0:CONV_HALO, :] = jnp.zeros((CONV_HALO, ext_v.shape[1]), F32)

    @pl.when(si > 0)
    def _():
        ext_g[0:CONV_HALO, :] = carry[f, 0]
        ext_v[0:CONV_HALO, :] = carry[f, 1]

    ext_g[CONV_HALO:CONV_HALO + tm, :] = up_g
    ext_v[CONV_HALO:CONV_HALO + tm, :] = up_v
    carry[f, 0] = up_g[tm - CONV_HALO:tm, :]
    carry[f, 1] = up_v[tm - CONV_HALO:tm, :]

    def causal_conv(ext, up, cw_ref, cb_ref):
        out = cb_ref[...]
        for j in range(CONV_WIDTH - 1):
            lag = CONV_WIDTH - 1 - j
            out = out + ext[CONV_HALO - lag:CONV_HALO - lag + tm, :] * cw_ref[j:j + 1, :]
        return out + up * cw_ref[CONV_WIDTH - 1:CONV_WIDTH, :]

    cg = causal_conv(ext_g, up_g, cwg_ref, cbg_ref)
    cv = causal_conv(ext_v, up_v, cwv_ref, cbv_ref)
    act = (cg * (1.0 / (1.0 + jnp.exp(-cg))) * cv).astype(BF16)
    o_ref[0] += _dot(act, wd_ref[...])

    if final_norm:
        @pl.when(f == pl.num_programs(2) - 1)
        def _():
            o_ref[0] = _rms_norm(o_ref[0], gfin_ref[...])


def _conv_ffn(x, g, w_up, conv_w, conv_b, w_down, g_final, *, final_norm, tm=512, tf=512):
    bsz = x.shape[0]
    nf = D_FF // tf
    row_spec = pl.BlockSpec((1, tm, D_MODEL), lambda b, i, f: (b, i, 0))
    w_up = w_up.astype(BF16)
    conv_b = conv_b.reshape(1, 2 * D_FF)
    return pl.pallas_call(
        functools.partial(_conv_ffn_kernel, tm=tm, final_norm=final_norm),
        grid=(bsz, SEQ // tm, nf),
        in_specs=[
            row_spec,
            pl.BlockSpec((1, D_MODEL), lambda b, i, f: (0, 0)),
            pl.BlockSpec((D_MODEL, tf), lambda b, i, f: (0, f)),
            pl.BlockSpec((D_MODEL, tf), lambda b, i, f: (0, nf + f)),
            pl.BlockSpec((CONV_WIDTH, tf), lambda b, i, f: (0, f)),
            pl.BlockSpec((CONV_WIDTH, tf), lambda b, i, f: (0, nf + f)),
            pl.BlockSpec((1, tf), lambda b, i, f: (0, f)),
            pl.BlockSpec((1, tf), lambda b, i, f: (0, nf + f)),
            pl.BlockSpec((tf, D_MODEL), lambda b, i, f: (f, 0)),
            pl.BlockSpec((1, D_MODEL), lambda b, i, f: (0, 0)),
        ],
        out_specs=row_spec,
        out_shape=jax.ShapeDtypeStruct(x.shape, F32),
        scratch_shapes=[
            pltpu.VMEM((tm, D_MODEL), BF16),
            pltpu.VMEM((tm + CONV_HALO, tf), F32),
            pltpu.VMEM((tm + CONV_HALO, tf), F32),
            pltpu.VMEM((nf, 2, CONV_HALO, tf), F32),
        ],
        compiler_params=_params(3),
        name="conv_ffn_final" if final_norm else "conv_ffn",
    )(x, g.reshape(1, D_MODEL), w_up, w_up, conv_w, conv_w, conv_b, conv_b,
      w_down.astype(BF16), g_final.reshape(1, D_MODEL))


def _log_sigmoid(x):
    return jnp.minimum(x, 0.0) - jnp.log1p(jnp.exp(-jnp.abs(x)))


def _qkv_kernel(x_ref, g_ref, w_ref, wf_ref, bf_ref, q_ref, k_ref, v_ref, c_ref, carry, *, tm):
    si = pl.program_id(1)
    h = _rms_norm(x_ref[0], g_ref[...]).astype(BF16)
    q_scale = C_HEAD_DIM ** -0.5
    for part, (o_ref, scale) in enumerate(((q_ref, q_scale), (k_ref, None), (v_ref, None))):
        p = _dot(h, w_ref[:, part * C_WIDTH:(part + 1) * C_WIDTH])
        if scale is not None:
            p = p * scale
        for hd in range(C_HEADS):
            o_ref[0, hd] = p[:, hd * C_HEAD_DIM:(hd + 1) * C_HEAD_DIM].astype(BF16)

    log_f = _log_sigmoid(_dot(h, wf_ref[...]) + bf_ref[...])
    row = lax.broadcasted_iota(jnp.int32, (tm, LANES), 0)
    k = 1
    while k < tm:
        log_f = log_f + jnp.where(row >= k, pltpu.roll(log_f, k, axis=0), 0.0)
        k *= 2

    @pl.when(si == 0)
    def _():
        carry[...] = jnp.zeros(carry.shape, F32)

    c = log_f + carry[0:1, :]
    c_ref[0] = c
    carry[...] = jnp.broadcast_to(c[tm - 1:tm, :], carry.shape)


def _qkv_proj(x, g, w_in, b_forget, *, tm=256):
    bsz = x.shape[0]
    w_qkv = w_in[:, :3 * C_WIDTH].astype(BF16)
    w_f = jnp.pad(w_in[:, 3 * C_WIDTH:], ((0, 0), (0, LANES - C_HEADS))).astype(BF16)
    b_f = jnp.pad(b_forget, (0, LANES - C_HEADS)).reshape(1, LANES)
    head_spec = pl.BlockSpec((1, C_HEADS, tm, C_HEAD_DIM), lambda b, i: (b, 0, i, 0))
    head_shape = jax.ShapeDtypeStruct((bsz, C_HEADS, SEQ, C_HEAD_DIM), BF16)
    return pl.pallas_call(
        functools.partial(_qkv_kernel, tm=tm),
        grid=(bsz, SEQ // tm),
        in_specs=[
            pl.BlockSpec((1, tm, D_MODEL), lambda b, i: (b, i, 0)),
            _resident((1, D_MODEL)),
            _resident((D_MODEL, 3 * C_WIDTH)),
            _resident((D_MODEL, LANES)),
            _resident((1, LANES)),
        ],
        out_specs=[head_spec, head_spec, head_spec,
                   pl.BlockSpec((1, tm, LANES), lambda b, i: (b, i, 0))],
        out_shape=[head_shape, head_shape, head_shape,
                   jax.ShapeDtypeStruct((bsz, SEQ, LANES), F32)],
        scratch_shapes=[pltpu.VMEM((SUBLANES, LANES), F32)],
        compiler_params=_params(2),
        name="qkv_proj",
    )(x, g.reshape(1, D_MODEL), w_qkv, w_f, b_f)


def _fox_attention_kernel(q_ref, k_ref, v_ref, ccol_ref, crow_ref, o_ref, *, tq):
    hd = pl.program_id(1)
    qi = pl.program_id(2)
    q = q_ref[0, 0]
    lane = lax.broadcasted_iota(jnp.int32, (tq, LANES), 1)
    c_q = jnp.sum(jnp.where(lane == hd, ccol_ref[0], 0.0), axis=1, keepdims=True)

    def tile(start, carry, diagonal):
        m, l, acc = carry
        k = k_ref[0, 0, pl.ds(start, tq), :]
        v = v_ref[0, 0, pl.ds(start, tq), :]
        s = lax.dot_general(q, k, (((1,), (1,)), ((), ())), preferred_element_type=F32)
        s = s + c_q - crow_ref[0, 0, :, pl.ds(start, tq)]
        if diagonal:
            r = lax.broadcasted_iota(jnp.int32, (tq, tq), 0)
            c = lax.broadcasted_iota(jnp.int32, (tq, tq), 1)
            s = jnp.where(r >= c, s, -jnp.inf)
        m_new = jnp.maximum(m, jnp.max(s, axis=1, keepdims=True))
        alpha = jnp.exp(m - m_new)
        p = jnp.exp(s - m_new)
        l = alpha * l + jnp.sum(p, axis=1, keepdims=True)
        acc = alpha * acc + _dot(p.astype(BF16), v)
        return m_new, l, acc

    init = (jnp.full((tq, 1), -jnp.inf, F32), jnp.zeros((tq, 1), F32),
            jnp.zeros((tq, C_HEAD_DIM), F32))
    carry = lax.fori_loop(
        0, qi, lambda kj, c: tile(pl.multiple_of(kj * tq, tq), c, False), init)
    _, l, acc = tile(pl.multiple_of(qi * tq, tq), carry, True)
    o_ref[0] = (acc / l).astype(BF16)


def _fox_attention(q, k, v, c, *, tq=256):
    bsz = q.shape[0]
    c_row = jnp.transpose(c[:, :, :C_HEADS], (0, 2, 1)).reshape(bsz, C_HEADS, 1, SEQ)
    kv_spec = pl.BlockSpec((1, 1, SEQ, C_HEAD_DIM), lambda b, h, i: (b, h, 0, 0))
    return pl.pallas_call(
        functools.partial(_fox_attention_kernel, tq=tq),
        grid=(bsz, C_HEADS, SEQ // tq),
        in_specs=[
            pl.BlockSpec((1, 1, tq, C_HEAD_DIM), lambda b, h, i: (b, h, i, 0)),
            kv_spec,
            kv_spec,
            pl.BlockSpec((1, tq, LANES), lambda b, h, i: (b, i, 0)),
            pl.BlockSpec((1, 1, 1, SEQ), lambda b, h, i: (b, h, 0, 0)),
        ],
        out_specs=pl.BlockSpec((1, tq, C_HEAD_DIM), lambda b, h, i: (b, i, h)),
        out_shape=jax.ShapeDtypeStruct((bsz, SEQ, C_WIDTH), BF16),
        compiler_params=_params(3),
        name="fox_attention",
    )(q, k, v, c, c_row)


def _proj_residual_kernel(x_ref, a_ref, w_ref, o_ref):
    o_ref[0] = x_ref[0] + _dot(a_ref[0], w_ref[...])


def _proj_residual(x, a, w, *, tm=512):
    bsz = x.shape[0]
    row_spec = pl.BlockSpec((1, tm, D_MODEL), lambda b, i: (b, i, 0))
    return pl.pallas_call(
        _proj_residual_kernel,
        grid=(bsz, SEQ // tm),
        in_specs=[row_spec, pl.BlockSpec((1, tm, a.shape[2]), lambda b, i: (b, i, 0)),
                  _resident(w.shape)],
        out_specs=row_spec,
        out_shape=jax.ShapeDtypeStruct(x.shape, F32),
        compiler_params=_params(2),
        name="proj_residual",
    )(x, a, w.astype(BF16))


def kernel(x, norm_mix_g, norm_ffn_g, final_norm_g, w_in_even, ln_v_g, ln_v_b, w_spatial,
           b_spatial, w_pool, pool_scale, w_out_even, w_in_odd, b_forget, w_out_odd, w_up,
           conv_w, conv_b, w_down):
    depth = norm_mix_g.shape[0]
    assert x.shape[1:] == (SEQ, D_MODEL)
    for i in range(depth):
        if i % 2 == 0:
            e = i // 2
            x = _even_mixer(x, norm_mix_g[i], w_in_even[e], ln_v_g[e], ln_v_b[e], w_spatial[e],
                            b_spatial[e], w_pool[e], pool_scale[e], w_out_even[e])
        else:
            o = i // 2
            q, k, v, c = _qkv_proj(x, norm_mix_g[i], w_in_odd[o], b_forget[o])
            att = _fox_attention(q, k, v, c)
            x = _proj_residual(x, att, w_out_odd[o])
        x = _conv_ffn(x, norm_ffn_g[i], w_up[i], conv_w[i], conv_b[i], w_down[i], final_norm_g,
                      final_norm=(i == depth - 1))
    if depth == 0:
        raise NotImplementedError("depth 0 is not a configuration of this problem")
    return x
```

```python
import functools

import jax
import jax.numpy as jnp
from jax import lax
from jax.experimental import pallas as pl
from jax.experimental.pallas import tpu as pltpu

D_MODEL = 2048
SEQ = 2048
EPS = 1e-6
CHUNK = 64
A_BLOCK = 128
A_GROUPS = 4
A_WIDTH = D_MODEL // 2
A_GROUP_DIM = A_WIDTH // A_GROUPS
B_WINDOWS = (2, 4, 8, 16)
B_WIDTH = D_MODEL // 2
B_GROUP_DIM = B_WIDTH // len(B_WINDOWS)
C_HEADS = 16
C_HEAD_DIM = 128
C_WIDTH = C_HEADS * C_HEAD_DIM
D_FF = 5632
CONV_WIDTH = 3

LANES = 128
SUBLANES = 8
VMEM_LIMIT_BYTES = 60 * 1024 * 1024

POOL_HALO = 16
CONV_HALO = SUBLANES
QC = 256

BF16 = jnp.bfloat16
F32 = jnp.float32
LOG2E = 1.4426950408889634


def _dot(a, b):
    return jnp.dot(a, b, preferred_element_type=F32)


def _rms_norm(x, g):
    return x * lax.rsqrt(jnp.mean(x * x, axis=-1, keepdims=True) + EPS) * g


def _gelu_exact(x):
    return 0.5 * x * (1.0 + lax.erf(x * (0.5 ** 0.5)))


def _resident(shape):
    zeros = (0,) * len(shape)
    return pl.BlockSpec(shape, lambda *_: zeros, pipeline_mode=pl.Buffered(1))


def _params(n_grid):
    return pltpu.CompilerParams(
        dimension_semantics=("arbitrary",) * n_grid,
        vmem_limit_bytes=VMEM_LIMIT_BYTES,
    )


def _even_mixer_kernel(x_ref, g_ref, win_ref, lng_ref, lnb_ref, ws_ref, bs_ref,
                       wpool_ref, pscale_ref, wout_ref, o_ref, pool_ext, mix_scr, *, tm):
    si = pl.program_id(1)
    x = x_ref[0]
    h = _rms_norm(x, g_ref[...]).astype(BF16)

    u = _gelu_exact(_dot(h, win_ref[:, 0:A_WIDTH]))
    v = _gelu_exact(_dot(h, win_ref[:, A_WIDTH:2 * A_WIDTH]))
    t_idx = lax.broadcasted_iota(jnp.int32, (A_BLOCK, A_BLOCK), 0) // CHUNK
    s_idx = lax.broadcasted_iota(jnp.int32, (A_BLOCK, A_BLOCK), 1) // CHUNK
    causal = t_idx >= s_idx
    for g in range(A_GROUPS):
        cols = slice(g * A_GROUP_DIM, (g + 1) * A_GROUP_DIM)
        vg = v[:, cols]
        mu = jnp.mean(vg, axis=-1, keepdims=True)
        var = jnp.mean(jnp.square(vg - mu), axis=-1, keepdims=True)
        vn = ((vg - mu) * lax.rsqrt(var + EPS) * lng_ref[:, cols] + lnb_ref[:, cols]).astype(BF16)
        w_g = jnp.where(causal, ws_ref[g], 0.0).astype(BF16)
        for n in range(tm // A_BLOCK):
            rows = slice(n * A_BLOCK, (n + 1) * A_BLOCK)
            gate = _dot(w_g, vn[rows]) + bs_ref[g]
            mix_scr[rows, cols] = (u[rows, cols] * gate).astype(BF16)

    @pl.when(si == 0)
    def _():
        pool_ext[0:POOL_HALO, :] = jnp.zeros((POOL_HALO, B_WIDTH), F32)

    @pl.when(si > 0)
    def _():
        pool_ext[0:POOL_HALO, :] = pool_ext[tm:tm + POOL_HALO, :]

    pool_ext[POOL_HALO:POOL_HALO + tm, :] = _dot(h, win_ref[:, 2 * A_WIDTH:2 * A_WIDTH + B_WIDTH])
    pos = si * tm + lax.broadcasted_iota(jnp.int32, (tm, 1), 0)
    for j, w in enumerate(B_WINDOWS):
        cols = slice(j * B_GROUP_DIM, (j + 1) * B_GROUP_DIM)
        xg = pool_ext[POOL_HALO:POOL_HALO + tm, cols]
        acc = xg
        for k in range(1, w):
            acc = acc + pool_ext[POOL_HALO - k:POOL_HALO - k + tm, cols]
        count = jnp.minimum(pos + 1, w).astype(F32)
        d = (acc / count - xg).astype(BF16)
        yb = _dot(d, wpool_ref[j]) * pscale_ref[:, cols]
        mix_scr[:, A_WIDTH + j * B_GROUP_DIM:A_WIDTH + (j + 1) * B_GROUP_DIM] = yb.astype(BF16)

    o_ref[0] = x + _dot(mix_scr[...], wout_ref[...])


def _even_mixer(x, g, w_in, ln_g, ln_b, w_s, b_s, w_pool, p_scale, w_out, *, tm=256):
    bsz = x.shape[0]
    even_in = 2 * A_WIDTH + B_WIDTH
    row_spec = pl.BlockSpec((1, tm, D_MODEL), lambda b, i: (b, i, 0))
    return pl.pallas_call(
        functools.partial(_even_mixer_kernel, tm=tm),
        grid=(bsz, SEQ // tm),
        in_specs=[
            row_spec,
            _resident((1, D_MODEL)),
            _resident((D_MODEL, even_in)),
            _resident((1, A_WIDTH)),
            _resident((1, A_WIDTH)),
            _resident((A_GROUPS, A_BLOCK, A_BLOCK)),
            _resident((A_GROUPS, A_BLOCK, 1)),
            _resident((len(B_WINDOWS), B_GROUP_DIM, B_GROUP_DIM)),
            _resident((1, B_WIDTH)),
            _resident((A_WIDTH + B_WIDTH, D_MODEL)),
        ],
        out_specs=row_spec,
        out_shape=jax.ShapeDtypeStruct(x.shape, F32),
        scratch_shapes=[
            pltpu.VMEM((tm + POOL_HALO, B_WIDTH), F32),
            pltpu.VMEM((tm, A_WIDTH + B_WIDTH), BF16),
        ],
        compiler_params=_params(2),
        name="even_mixer",
    )(x, g.reshape(1, D_MODEL), w_in.astype(BF16), ln_g.reshape(1, A_WIDTH),
      ln_b.reshape(1, A_WIDTH), w_s, b_s.reshape(A_GROUPS, A_BLOCK, 1),
      w_pool.astype(BF16), p_scale.reshape(1, B_WIDTH), w_out.astype(BF16))


def _conv_ffn_kernel(x_ref, g_ref, wg_ref, wv_ref, cwg_ref, cwv_ref, cbg_ref, cbv_ref,
                     wd_ref, gfin_ref, o_ref, h_scr, ext_g, ext_v, carry, *, tm, final_norm):
    si = pl.program_id(1)
    f = pl.program_id(2)

    @pl.when(f == 0)
    def _():
        x = x_ref[0]
        h_scr[...] = _rms_norm(x, g_ref[...]).astype(BF16)
        o_ref[0] = x

    h = h_scr[...]
    up_g = _dot(h, wg_ref[...])
    up_v = _dot(h, wv_ref[...])

    @pl.when(si == 0)
    def _():
        ext_g[0:CONV_HALO, :] = jnp.zeros((CONV_HALO, ext_g.shape[1]), F32)
        ext_v[0:CONV_HALO, :] = jnp.zeros((CONV_HALO, ext_v.shape[1]), F32)

    @pl.when(si > 0)
    def _():
        ext_g[0:CONV_HALO, :] = carry[f, 0]
        ext_v[0:CONV_HALO, :] = carry[f, 1]

    ext_g[CONV_HALO:CONV_HALO + tm, :] = up_g
    ext_v[CONV_HALO:CONV_HALO + tm, :] = up_v
    carry[f, 0] = up_g[tm - CONV_HALO:tm, :]
    carry[f, 1] = up_v[tm - CONV_HALO:tm, :]

    def causal_conv(ext, up, cw_ref, cb_ref):
        out = cb_ref[...]
        for j in range(CONV_WIDTH - 1):
            lag = CONV_WIDTH - 1 - j
            out = out + ext[CONV_HALO - lag:CONV_HALO - lag + tm, :] * cw_ref[j:j + 1, :]
        return out + up * cw_ref[CONV_WIDTH - 1:CONV_WIDTH, :]

    cg = causal_conv(ext_g, up_g, cwg_ref, cbg_ref)
    cv = causal_conv(ext_v, up_v, cwv_ref, cbv_ref)
    act = (cg * (1.0 / (1.0 + jnp.exp(-cg))) * cv).astype(BF16)
    o_ref[0] += _dot(act, wd_ref[...])

    if final_norm:
        @pl.when(f == pl.num_programs(2) - 1)
        def _():
            o_ref[0] = _rms_norm(o_ref[0], gfin_ref[...])


def _conv_ffn(x, g, w_up, conv_w, conv_b, w_down, g_final, *, final_norm, tm=512, tf=512):
    bsz = x.shape[0]
    nf = D_FF // tf
    row_spec = pl.BlockSpec((1, tm, D_MODEL), lambda b, i, f: (b, i, 0))
    w_up = w_up.astype(BF16)
    conv_b = conv_b.reshape(1, 2 * D_FF)
    return pl.pallas_call(
        functools.partial(_conv_ffn_kernel, tm=tm, final_norm=final_norm),
        grid=(bsz, SEQ // tm, nf),
        in_specs=[
            row_spec,
            pl.BlockSpec((1, D_MODEL), lambda b, i, f: (0, 0)),
            pl.BlockSpec((D_MODEL, tf), lambda b, i, f: (0, f)),
            pl.BlockSpec((D_MODEL, tf), lambda b, i, f: (0, nf + f)),
            pl.BlockSpec((CONV_WIDTH, tf), lambda b, i, f: (0, f)),
            pl.BlockSpec((CONV_WIDTH, tf), lambda b, i, f: (0, nf + f)),
            pl.BlockSpec((1, tf), lambda b, i, f: (0, f)),
            pl.BlockSpec((1, tf), lambda b, i, f: (0, nf + f)),
            pl.BlockSpec((tf, D_MODEL), lambda b, i, f: (f, 0)),
            pl.BlockSpec((1, D_MODEL), lambda b, i, f: (0, 0)),
        ],
        out_specs=row_spec,
        out_shape=jax.ShapeDtypeStruct(x.shape, F32),
        scratch_shapes=[
            pltpu.VMEM((tm, D_MODEL), BF16),
            pltpu.VMEM((tm + CONV_HALO, tf), F32),
            pltpu.VMEM((tm + CONV_HALO, tf), F32),
            pltpu.VMEM((nf, 2, CONV_HALO, tf), F32),
        ],
        compiler_params=_params(3),
        name="conv_ffn_final" if final_norm else "conv_ffn",
    )(x, g.reshape(1, D_MODEL), w_up, w_up, conv_w, conv_w, conv_b, conv_b,
      w_down.astype(BF16), g_final.reshape(1, D_MODEL))


def _log_sigmoid(x):
    return jnp.minimum(x, 0.0) - jnp.log1p(jnp.exp(-jnp.abs(x)))


def _qkv_kernel(x_ref, g_ref, wqt_ref, wk_ref, wvt_ref, wf_ref, bf_ref,
                qt_ref, k_ref, vt_ref, c_ref, carry, *, tm):
    si = pl.program_id(1)
    hf = _rms_norm(x_ref[0], g_ref[...])
    h = hf.astype(BF16)
    ht = hf.T.astype(BF16)
    q_scale = C_HEAD_DIM ** -0.5 * LOG2E
    kk = _dot(h, wk_ref[...])
    qt = _dot(wqt_ref[...], ht) * q_scale
    vt = _dot(wvt_ref[...], ht)
    for hd in range(C_HEADS):
        dims = slice(hd * C_HEAD_DIM, (hd + 1) * C_HEAD_DIM)
        k_ref[0, hd] = kk[:, dims].astype(BF16)
        qt_ref[0, hd, 0] = qt[dims, :].astype(BF16)
        vt_ref[0, hd, 0] = vt[dims, :].astype(BF16)

    log_f = _log_sigmoid(_dot(h, wf_ref[...]) + bf_ref[...])
    row = lax.broadcasted_iota(jnp.int32, (tm, LANES), 0)
    k = 1
    while k < tm:
        log_f = log_f + jnp.where(row >= k, pltpu.roll(log_f, k, axis=0), 0.0)
        k *= 2

    @pl.when(si == 0)
    def _():
        carry[...] = jnp.zeros(carry.shape, F32)

    c = log_f + carry[0:1, :]
    c_ref[0] = c
    carry[...] = jnp.broadcast_to(c[tm - 1:tm, :], carry.shape)


def _qkv_proj(x, g, w_in, b_forget):
    bsz = x.shape[0]
    tm = QC
    wq_t = w_in[:, :C_WIDTH].T.astype(BF16)
    wk = w_in[:, C_WIDTH:2 * C_WIDTH].astype(BF16)
    wv_t = w_in[:, 2 * C_WIDTH:3 * C_WIDTH].T.astype(BF16)
    w_f = jnp.pad(w_in[:, 3 * C_WIDTH:], ((0, 0), (0, LANES - C_HEADS))).astype(BF16)
    b_f = jnp.pad(b_forget, (0, LANES - C_HEADS)).reshape(1, LANES)
    t_spec = pl.BlockSpec((1, C_HEADS, 1, C_HEAD_DIM, QC), lambda b, i: (b, 0, i, 0, 0))
    t_shape = jax.ShapeDtypeStruct((bsz, C_HEADS, SEQ // QC, C_HEAD_DIM, QC), BF16)
    return pl.pallas_call(
        functools.partial(_qkv_kernel, tm=tm),
        grid=(bsz, SEQ // tm),
        in_specs=[
            pl.BlockSpec((1, tm, D_MODEL), lambda b, i: (b, i, 0)),
            _resident((1, D_MODEL)),
            _resident((C_WIDTH, D_MODEL)),
            _resident((D_MODEL, C_WIDTH)),
            _resident((C_WIDTH, D_MODEL)),
            _resident((D_MODEL, LANES)),
            _resident((1, LANES)),
        ],
        out_specs=[t_spec,
                   pl.BlockSpec((1, C_HEADS, tm, C_HEAD_DIM), lambda b, i: (b, 0, i, 0)),
                   t_spec,
                   pl.BlockSpec((1, tm, LANES), lambda b, i: (b, i, 0))],
        out_shape=[t_shape,
                   jax.ShapeDtypeStruct((bsz, C_HEADS, SEQ, C_HEAD_DIM), BF16),
                   t_shape,
                   jax.ShapeDtypeStruct((bsz, SEQ, LANES), F32)],
        scratch_shapes=[pltpu.VMEM((SUBLANES, LANES), F32)],
        compiler_params=_params(2),
        name="qkv_proj",
    )(x, g.reshape(1, D_MODEL), wq_t, wk, wv_t, w_f, b_f)


def _fox_attention_kernel(qt_ref, k_ref, vt_ref, c_ref, crow_ref, o_ref,
                          ck_scr, m_scr, l_scr, acc_scr, *, tq, nh):
    hp = pl.program_id(1)
    qi = pl.program_id(2)
    nc = tq // QC

    @pl.when(qi == 0)
    def _():
        lane = lax.broadcasted_iota(jnp.int32, (SEQ, LANES), 1)
        c_all = c_ref[0] * LOG2E
        for j in range(nh):
            col = jnp.sum(jnp.where(lane == hp * nh + j, c_all, 0.0), axis=1, keepdims=True)
            ck_scr[j] = jnp.broadcast_to(col, (SEQ, QC))

    m_scr[...] = jnp.full(m_scr.shape, -jnp.inf, F32)
    l_scr[...] = jnp.zeros(l_scr.shape, F32)
    acc_scr[...] = jnp.zeros(acc_scr.shape, F32)

    def scores(j, c, kt):
        ks = pl.multiple_of(kt * QC, QC)
        return _dot(k_ref[0, j, pl.ds(ks, QC), :], qt_ref[0, j, c])

    def update(j, c, kt, diagonal, t):
        cs = slice(c * QC, (c + 1) * QC)
        ks = pl.multiple_of(kt * QC, QC)
        t = t - ck_scr[j, pl.ds(ks, QC), :]
        if diagonal:
            key = lax.broadcasted_iota(jnp.int32, (QC, QC), 0)
            qry = lax.broadcasted_iota(jnp.int32, (QC, QC), 1)
            t = jnp.where(key <= qry, t, -jnp.inf)
        c_q = crow_ref[0, j, :, cs] * LOG2E
        m_old = m_scr[j, :, cs]
        m_new = jnp.maximum(m_old, c_q + jnp.max(t, axis=0, keepdims=True))
        alpha = jnp.exp2(m_old - m_new)
        p = jnp.exp2(t + (c_q - m_new))
        l_scr[j, :, cs] = alpha * l_scr[j, :, cs] + jnp.sum(p, axis=0, keepdims=True)
        pv = _dot(vt_ref[0, j, kt], p.astype(BF16))
        acc_scr[j, :, cs] = alpha * acc_scr[j, :, cs] + pv
        m_scr[j, :, cs] = m_new

    def run(blocks):
        ahead = 4
        ts = [scores(*b[:3]) for b in blocks[:ahead]]
        for n, (j, c, kt, diagonal) in enumerate(blocks):
            if n + ahead < len(blocks):
                ts.append(scores(*blocks[n + ahead][:3]))
            update(j, c, kt, diagonal, ts[n])

    def full_tile(kj, carry):
        run([(j, c, kj * nc + s, False)
             for s in range(nc) for j in range(nh) for c in range(nc)])
        return carry

    lax.fori_loop(0, qi, full_tile, 0)
    run([(j, c, qi * nc + s, c == s)
         for s in range(nc) for j in range(nh) for c in range(s, nc)])
    for j in range(nh):
        o = acc_scr[j] / l_scr[j]
        o_ref[0, :, j * C_HEAD_DIM:(j + 1) * C_HEAD_DIM] = o.T.astype(BF16)


def _fox_attention(qt, k, vt, c, *, tq=512, nh=2):
    bsz = k.shape[0]
    nc = tq // QC
    c_row = jnp.transpose(c[:, :, :C_HEADS], (0, 2, 1)).reshape(bsz, C_HEADS, 1, SEQ)
    return pl.pallas_call(
        functools.partial(_fox_attention_kernel, tq=tq, nh=nh),
        grid=(bsz, C_HEADS // nh, SEQ // tq),
        in_specs=[
            pl.BlockSpec((1, nh, nc, C_HEAD_DIM, QC), lambda b, h, i: (b, h, i, 0, 0)),
            pl.BlockSpec((1, nh, SEQ, C_HEAD_DIM), lambda b, h, i: (b, h, 0, 0)),
            pl.BlockSpec((1, nh, SEQ // QC, C_HEAD_DIM, QC), lambda b, h, i: (b, h, 0, 0, 0)),
            pl.BlockSpec((1, SEQ, LANES), lambda b, h, i: (b, 0, 0)),
            pl.BlockSpec((1, nh, 1, tq), lambda b, h, i: (b, h, 0, i)),
        ],
        out_specs=pl.BlockSpec((1, tq, nh * C_HEAD_DIM), lambda b, h, i: (b, i, h)),
        out_shape=jax.ShapeDtypeStruct((bsz, SEQ, C_WIDTH), BF16),
        scratch_shapes=[
            pltpu.VMEM((nh, SEQ, QC), F32),
            pltpu.VMEM((nh, 1, tq), F32),
            pltpu.VMEM((nh, 1, tq), F32),
            pltpu.VMEM((nh, C_HEAD_DIM, tq), F32),
        ],
        compiler_params=_params(3),
        name="fox_attention",
    )(qt, k, vt, c, c_row)


def _proj_residual_kernel(x_ref, a_ref, w_ref, o_ref):
    o_ref[0] = x_ref[0] + _dot(a_ref[0], w_ref[...])


def _proj_residual(x, a, w, *, tm=512):
    bsz = x.shape[0]
    row_spec = pl.BlockSpec((1, tm, D_MODEL), lambda b, i: (b, i, 0))
    return pl.pallas_call(
        _proj_residual_kernel,
        grid=(bsz, SEQ // tm),
        in_specs=[row_spec, pl.BlockSpec((1, tm, a.shape[2]), lambda b, i: (b, i, 0)),
                  _resident(w.shape)],
        out_specs=row_spec,
        out_shape=jax.ShapeDtypeStruct(x.shape, F32),
        compiler_params=_params(2),
        name="proj_residual",
    )(x, a, w.astype(BF16))


def kernel(x, norm_mix_g, norm_ffn_g, final_norm_g, w_in_even, ln_v_g, ln_v_b, w_spatial,
           b_spatial, w_pool, pool_scale, w_out_even, w_in_odd, b_forget, w_out_odd, w_up,
           conv_w, conv_b, w_down):
    depth = norm_mix_g.shape[0]
    assert x.shape[1:] == (SEQ, D_MODEL)
    for i in range(depth):
        if i % 2 == 0:
            e = i // 2
            x = _even_mixer(x, norm_mix_g[i], w_in_even[e], ln_v_g[e], ln_v_b[e], w_spatial[e],
                            b_spatial[e], w_pool[e], pool_scale[e], w_out_even[e])
        else:
            o = i // 2
            qt, k, vt, c = _qkv_proj(x, norm_mix_g[i], w_in_odd[o], b_forget[o])
            att = _fox_attention(qt, k, vt, c)
            x = _proj_residual(x, att, w_out_odd[o])
        x = _conv_ffn(x, norm_ffn_g[i], w_up[i], conv_w[i], conv_b[i], w_down[i], final_norm_g,
                      final_norm=(i == depth - 1))
    if depth == 0:
        raise NotImplementedError("depth 0 is not a configuration of this problem")
    return x
```

```python
import functools

import jax
import jax.numpy as jnp
from jax import lax
from jax.experimental import pallas as pl
from jax.experimental.pallas import tpu as pltpu

D_MODEL = 2048
SEQ = 2048
EPS = 1e-6
CHUNK = 64
A_BLOCK = 128
A_GROUPS = 4
A_WIDTH = D_MODEL // 2
A_GROUP_DIM = A_WIDTH // A_GROUPS
B_WINDOWS = (2, 4, 8, 16)
B_WIDTH = D_MODEL // 2
B_GROUP_DIM = B_WIDTH // len(B_WINDOWS)
C_HEADS = 16
C_HEAD_DIM = 128
C_WIDTH = C_HEADS * C_HEAD_DIM
D_FF = 5632
CONV_WIDTH = 3

LANES = 128
SUBLANES = 8
VMEM_LIMIT_BYTES = 60 * 1024 * 1024

POOL_HALO = 16
CONV_HALO = SUBLANES
FFN_ROWS = 256
QC = 256

BF16 = jnp.bfloat16
F32 = jnp.float32
LOG2E = 1.4426950408889634


def _dot(a, b):
    return jnp.dot(a, b, preferred_element_type=F32)


def _rms_norm(x, g):
    return x * lax.rsqrt(jnp.mean(x * x, axis=-1, keepdims=True) + EPS) * g


def _gelu_exact(x):
    return 0.5 * x * (1.0 + lax.erf(x * (0.5 ** 0.5)))


def _resident(shape):
    zeros = (0,) * len(shape)
    return pl.BlockSpec(shape, lambda *_: zeros, pipeline_mode=pl.Buffered(1))


def _params(n_grid):
    return pltpu.CompilerParams(
        dimension_semantics=("arbitrary",) * n_grid,
        vmem_limit_bytes=VMEM_LIMIT_BYTES,
    )


def _even_mixer_kernel(x_ref, g_ref, win_ref, lng_ref, lnb_ref, ws_ref, bs_ref,
                       wpool_ref, pscale_ref, wout_ref, o_ref, pool_ext, mix_scr, *, tm):
    si = pl.program_id(1)
    x = x_ref[0]
    h = _rms_norm(x, g_ref[...]).astype(BF16)

    u = _gelu_exact(_dot(h, win_ref[:, 0:A_WIDTH]))
    v = _gelu_exact(_dot(h, win_ref[:, A_WIDTH:2 * A_WIDTH]))
    t_idx = lax.broadcasted_iota(jnp.int32, (A_BLOCK, A_BLOCK), 0) // CHUNK
    s_idx = lax.broadcasted_iota(jnp.int32, (A_BLOCK, A_BLOCK), 1) // CHUNK
    causal = t_idx >= s_idx
    for g in range(A_GROUPS):
        cols = slice(g * A_GROUP_DIM, (g + 1) * A_GROUP_DIM)
        vg = v[:, cols]
        mu = jnp.mean(vg, axis=-1, keepdims=True)
        var = jnp.mean(jnp.square(vg - mu), axis=-1, keepdims=True)
        vn = ((vg - mu) * lax.rsqrt(var + EPS) * lng_ref[:, cols] + lnb_ref[:, cols]).astype(BF16)
        w_g = jnp.where(causal, ws_ref[g], 0.0).astype(BF16)
        for n in range(tm // A_BLOCK):
            rows = slice(n * A_BLOCK, (n + 1) * A_BLOCK)
            gate = _dot(w_g, vn[rows]) + bs_ref[g]
            mix_scr[rows, cols] = (u[rows, cols] * gate).astype(BF16)

    @pl.when(si == 0)
    def _():
        pool_ext[0:POOL_HALO, :] = jnp.zeros((POOL_HALO, B_WIDTH), F32)

    @pl.when(si > 0)
    def _():
        pool_ext[0:POOL_HALO, :] = pool_ext[tm:tm + POOL_HALO, :]

    pool_ext[POOL_HALO:POOL_HALO + tm, :] = _dot(h, win_ref[:, 2 * A_WIDTH:2 * A_WIDTH + B_WIDTH])
    pos = si * tm + lax.broadcasted_iota(jnp.int32, (tm, 1), 0)
    for j, w in enumerate(B_WINDOWS):
        cols = slice(j * B_GROUP_DIM, (j + 1) * B_GROUP_DIM)
        xg = pool_ext[POOL_HALO:POOL_HALO + tm, cols]
        acc = xg
        for k in range(1, w):
            acc = acc + pool_ext[POOL_HALO - k:POOL_HALO - k + tm, cols]
        count = jnp.minimum(pos + 1, w).astype(F32)
        d = (acc / count - xg).astype(BF16)
        yb = _dot(d, wpool_ref[j]) * pscale_ref[:, cols]
        mix_scr[:, A_WIDTH + j * B_GROUP_DIM:A_WIDTH + (j + 1) * B_GROUP_DIM] = yb.astype(BF16)

    o_ref[0] = x + _dot(mix_scr[...], wout_ref[...])


def _even_mixer(x, g, w_in, ln_g, ln_b, w_s, b_s, w_pool, p_scale, w_out, *, tm=256):
    bsz = x.shape[0]
    even_in = 2 * A_WIDTH + B_WIDTH
    row_spec = pl.BlockSpec((1, tm, D_MODEL), lambda b, i: (b, i, 0))
    return pl.pallas_call(
        functools.partial(_even_mixer_kernel, tm=tm),
        grid=(bsz, SEQ // tm),
        in_specs=[
            row_spec,
            _resident((1, D_MODEL)),
            _resident((D_MODEL, even_in)),
            _resident((1, A_WIDTH)),
            _resident((1, A_WIDTH)),
            _resident((A_GROUPS, A_BLOCK, A_BLOCK)),
            _resident((A_GROUPS, A_BLOCK, 1)),
            _resident((len(B_WINDOWS), B_GROUP_DIM, B_GROUP_DIM)),
            _resident((1, B_WIDTH)),
            _resident((A_WIDTH + B_WIDTH, D_MODEL)),
        ],
        out_specs=row_spec,
        out_shape=jax.ShapeDtypeStruct(x.shape, F32),
        scratch_shapes=[
            pltpu.VMEM((tm + POOL_HALO, B_WIDTH), F32),
            pltpu.VMEM((tm, A_WIDTH + B_WIDTH), BF16),
        ],
        compiler_params=_params(2),
        name="even_mixer",
    )(x, g.reshape(1, D_MODEL), w_in.astype(BF16), ln_g.reshape(1, A_WIDTH),
      ln_b.reshape(1, A_WIDTH), w_s, b_s.reshape(A_GROUPS, A_BLOCK, 1),
      w_pool.astype(BF16), p_scale.reshape(1, B_WIDTH), w_out.astype(BF16))


def _conv_ffn_kernel(x_ref, g_ref, wg_ref, wv_ref, cwg_ref, cwv_ref, cbg_ref, cbv_ref,
                     wd_ref, *rest, tm, nf, final_norm):
    if final_norm:
        gfin_ref, o_ref, h_scr, ext_a, ext_b, carry = rest
    else:
        o_ref, h_scr, ext_a, ext_b, carry = rest
    ext = (ext_a, ext_b)
    b = pl.program_id(0)
    si = pl.program_id(1)
    f = pl.program_id(2)

    row_blocks = [slice(r, r + FFN_ROWS) for r in range(0, tm, FFN_ROWS)]

    def up_proj(ext_w, rows):
        h = h_scr[rows, :]
        for part, w_ref in enumerate((wg_ref, wv_ref)):
            up = _dot(h, w_ref[...])
            if rows.start == 0:
                ext_w[part, 0:CONV_HALO, :] = jnp.where(si > 0, carry[f, part], 0.0)
            ext_w[part, CONV_HALO + rows.start:CONV_HALO + rows.stop, :] = up
            if rows.stop == tm:
                carry[f, part] = up[FFN_ROWS - CONV_HALO:FFN_ROWS, :]

    def gated_act(ext_r, rows):
        def causal_conv(part, cw_ref, cb_ref):
            out = cb_ref[...]
            for j in range(CONV_WIDTH):
                lag = CONV_WIDTH - 1 - j
                lo = CONV_HALO + rows.start - lag
                out = out + ext_r[part, lo:lo + FFN_ROWS, :] * cw_ref[j:j + 1, :]
            return out

        cg = causal_conv(0, cwg_ref, cbg_ref)
        cv = causal_conv(1, cwv_ref, cbv_ref)
        return (cg * (1.0 / (1.0 + jnp.exp(-cg))) * cv).astype(BF16)

    def step(p, *, up, down, first=False, last=False):
        for rows in row_blocks:
            if first:
                x = x_ref[0, rows, :]
                h_scr[rows, :] = _rms_norm(x, g_ref[...]).astype(BF16)
                o_ref[0, rows, :] = x
            if up:
                up_proj(ext[p], rows)
            if down:
                out = o_ref[0, rows, :] + _dot(gated_act(ext[1 - p], rows), wd_ref[...])
                if last and final_norm:
                    out = _rms_norm(out, gfin_ref[...])
                o_ref[0, rows, :] = out

    @pl.when(f == 0)
    def _():
        @pl.when((b == 0) & (si == 0))
        def _():
            carry[...] = jnp.zeros(carry.shape, F32)

        step(0, up=True, down=False, first=True)

    for p in range(2):
        @pl.when((f >= 1) & (f < nf) & (f % 2 == p))
        def _():
            step(p, up=True, down=True)

    @pl.when(f == nf)
    def _():
        step(nf % 2, up=False, down=True, last=True)


def _conv_ffn(x, g, w_up, conv_w, conv_b, w_down, g_final=None, *, tm=512, tf=512):
    bsz = x.shape[0]
    nf = D_FF // tf
    final_norm = g_final is not None
    row_spec = pl.BlockSpec((1, tm, D_MODEL), lambda b, i, f: (b, i, 0))
    w_up = w_up.astype(BF16)
    conv_b = conv_b.reshape(1, 2 * D_FF)

    def chunk(f, lag):
        return jnp.clip(f - lag, 0, nf - 1)

    in_specs = [
        row_spec,
        pl.BlockSpec((1, D_MODEL), lambda b, i, f: (0, 0)),
        pl.BlockSpec((D_MODEL, tf), lambda b, i, f: (0, chunk(f, 0))),
        pl.BlockSpec((D_MODEL, tf), lambda b, i, f: (0, nf + chunk(f, 0))),
        pl.BlockSpec((CONV_WIDTH, tf), lambda b, i, f: (0, chunk(f, 1))),
        pl.BlockSpec((CONV_WIDTH, tf), lambda b, i, f: (0, nf + chunk(f, 1))),
        pl.BlockSpec((1, tf), lambda b, i, f: (0, chunk(f, 1))),
        pl.BlockSpec((1, tf), lambda b, i, f: (0, nf + chunk(f, 1))),
        pl.BlockSpec((tf, D_MODEL), lambda b, i, f: (chunk(f, 1), 0)),
    ]
    args = [x, g.reshape(1, D_MODEL), w_up, w_up, conv_w, conv_w, conv_b, conv_b,
            w_down.astype(BF16)]
    if final_norm:
        in_specs.append(pl.BlockSpec((1, D_MODEL), lambda b, i, f: (0, 0)))
        args.append(g_final.reshape(1, D_MODEL))
    return pl.pallas_call(
        functools.partial(_conv_ffn_kernel, tm=tm, nf=nf, final_norm=final_norm),
        grid=(bsz, SEQ // tm, nf + 1),
        in_specs=in_specs,
        out_specs=row_spec,
        out_shape=jax.ShapeDtypeStruct(x.shape, F32),
        scratch_shapes=[
            pltpu.VMEM((tm, D_MODEL), BF16),
            pltpu.VMEM((2, tm + CONV_HALO, tf), F32),
            pltpu.VMEM((2, tm + CONV_HALO, tf), F32),
            pltpu.VMEM((nf, 2, CONV_HALO, tf), F32),
        ],
        compiler_params=_params(3),
        name="conv_ffn_final" if final_norm else "conv_ffn",
    )(*args)


def _log_sigmoid(x):
    return jnp.minimum(x, 0.0) - jnp.log1p(jnp.exp(-jnp.abs(x)))


def _qkv_kernel(x_ref, g_ref, wqt_ref, wk_ref, wvt_ref, wf_ref, bf_ref,
                qt_ref, k_ref, vt_ref, c_ref, carry, *, tm):
    si = pl.program_id(1)
    hf = _rms_norm(x_ref[0], g_ref[...])
    h = hf.astype(BF16)
    ht = hf.T.astype(BF16)
    q_scale = C_HEAD_DIM ** -0.5 * LOG2E
    kk = _dot(h, wk_ref[...])
    qt = _dot(wqt_ref[...], ht) * q_scale
    vt = _dot(wvt_ref[...], ht)
    for hd in range(C_HEADS):
        dims = slice(hd * C_HEAD_DIM, (hd + 1) * C_HEAD_DIM)
        k_ref[0, hd] = kk[:, dims].astype(BF16)
        qt_ref[0, hd, 0] = qt[dims, :].astype(BF16)
        vt_ref[0, hd, 0] = vt[dims, :].astype(BF16)

    log_f = _log_sigmoid(_dot(h, wf_ref[...]) + bf_ref[...])
    row = lax.broadcasted_iota(jnp.int32, (tm, LANES), 0)
    k = 1
    while k < tm:
        log_f = log_f + jnp.where(row >= k, pltpu.roll(log_f, k, axis=0), 0.0)
        k *= 2

    @pl.when(si == 0)
    def _():
        carry[...] = jnp.zeros(carry.shape, F32)

    c = log_f + carry[0:1, :]
    c_ref[0] = c
    carry[...] = jnp.broadcast_to(c[tm - 1:tm, :], carry.shape)


def _qkv_proj(x, g, w_in, b_forget):
    bsz = x.shape[0]
    tm = QC
    wq_t = w_in[:, :C_WIDTH].T.astype(BF16)
    wk = w_in[:, C_WIDTH:2 * C_WIDTH].astype(BF16)
    wv_t = w_in[:, 2 * C_WIDTH:3 * C_WIDTH].T.astype(BF16)
    w_f = jnp.pad(w_in[:, 3 * C_WIDTH:], ((0, 0), (0, LANES - C_HEADS))).astype(BF16)
    b_f = jnp.pad(b_forget, (0, LANES - C_HEADS)).reshape(1, LANES)
    t_spec = pl.BlockSpec((1, C_HEADS, 1, C_HEAD_DIM, QC), lambda b, i: (b, 0, i, 0, 0))
    t_shape = jax.ShapeDtypeStruct((bsz, C_HEADS, SEQ // QC, C_HEAD_DIM, QC), BF16)
    return pl.pallas_call(
        functools.partial(_qkv_kernel, tm=tm),
        grid=(bsz, SEQ // tm),
        in_specs=[
            pl.BlockSpec((1, tm, D_MODEL), lambda b, i: (b, i, 0)),
            _resident((1, D_MODEL)),
            _resident((C_WIDTH, D_MODEL)),
            _resident((D_MODEL, C_WIDTH)),
            _resident((C_WIDTH, D_MODEL)),
            _resident((D_MODEL, LANES)),
            _resident((1, LANES)),
        ],
        out_specs=[t_spec,
                   pl.BlockSpec((1, C_HEADS, tm, C_HEAD_DIM), lambda b, i: (b, 0, i, 0)),
                   t_spec,
                   pl.BlockSpec((1, tm, LANES), lambda b, i: (b, i, 0))],
        out_shape=[t_shape,
                   jax.ShapeDtypeStruct((bsz, C_HEADS, SEQ, C_HEAD_DIM), BF16),
                   t_shape,
                   jax.ShapeDtypeStruct((bsz, SEQ, LANES), F32)],
        scratch_shapes=[pltpu.VMEM((SUBLANES, LANES), F32)],
        compiler_params=_params(2),
        name="qkv_proj",
    )(x, g.reshape(1, D_MODEL), wq_t, wk, wv_t, w_f, b_f)


def _fox_attention_kernel(qt_ref, k_ref, vt_ref, c_ref, crow_ref, o_ref,
                          ck_scr, m_scr, l_scr, acc_scr, *, tq, nh):
    hp = pl.program_id(1)
    qi = pl.program_id(2)
    nc = tq // QC

    @pl.when(qi == 0)
    def _():
        lane = lax.broadcasted_iota(jnp.int32, (SEQ, LANES), 1)
        c_all = c_ref[0] * LOG2E
        for j in range(nh):
            col = jnp.sum(jnp.where(lane == hp * nh + j, c_all, 0.0), axis=1, keepdims=True)
            ck_scr[j] = jnp.broadcast_to(col, (SEQ, QC))

    m_scr[...] = jnp.full(m_scr.shape, -jnp.inf, F32)
    l_scr[...] = jnp.zeros(l_scr.shape, F32)
    acc_scr[...] = jnp.zeros(acc_scr.shape, F32)

    def scores(j, c, kt):
        ks = pl.multiple_of(kt * QC, QC)
        return _dot(k_ref[0, j, pl.ds(ks, QC), :], qt_ref[0, j, c])

    def update(j, c, kt, diagonal, t):
        cs = slice(c * QC, (c + 1) * QC)
        ks = pl.multiple_of(kt * QC, QC)
        t = t - ck_scr[j, pl.ds(ks, QC), :]
        if diagonal:
            key = lax.broadcasted_iota(jnp.int32, (QC, QC), 0)
            qry = lax.broadcasted_iota(jnp.int32, (QC, QC), 1)
            t = jnp.where(key <= qry, t, -jnp.inf)
        c_q = crow_ref[0, j, :, cs] * LOG2E
        m_old = m_scr[j, :, cs]
        m_new = jnp.maximum(m_old, c_q + jnp.max(t, axis=0, keepdims=True))
        alpha = jnp.exp2(m_old - m_new)
        p = jnp.exp2(t + (c_q - m_new))
        l_scr[j, :, cs] = alpha * l_scr[j, :, cs] + jnp.sum(p, axis=0, keepdims=True)
        pv = _dot(vt_ref[0, j, kt], p.astype(BF16))
        acc_scr[j, :, cs] = alpha * acc_scr[j, :, cs] + pv
        m_scr[j, :, cs] = m_new

    def run(blocks):
        ahead = 4
        ts = [scores(*b[:3]) for b in blocks[:ahead]]
        for n, (j, c, kt, diagonal) in enumerate(blocks):
            if n + ahead < len(blocks):
                ts.append(scores(*blocks[n + ahead][:3]))
            update(j, c, kt, diagonal, ts[n])

    def full_tile(kj, carry):
        run([(j, c, kj * nc + s, False)
             for s in range(nc) for j in range(nh) for c in range(nc)])
        return carry

    lax.fori_loop(0, qi, full_tile, 0)
    run([(j, c, qi * nc + s, c == s)
         for s in range(nc) for j in range(nh) for c in range(s, nc)])
    for j in range(nh):
        o = acc_scr[j] / l_scr[j]
        o_ref[0, :, j * C_HEAD_DIM:(j + 1) * C_HEAD_DIM] = o.T.astype(BF16)


def _fox_attention(qt, k, vt, c, *, tq=512, nh=2):
    bsz = k.shape[0]
    nc = tq // QC
    c_row = jnp.transpose(c[:, :, :C_HEADS], (0, 2, 1)).reshape(bsz, C_HEADS, 1, SEQ)
    return pl.pallas_call(
        functools.partial(_fox_attention_kernel, tq=tq, nh=nh),
        grid=(bsz, C_HEADS // nh, SEQ // tq),
        in_specs=[
            pl.BlockSpec((1, nh, nc, C_HEAD_DIM, QC), lambda b, h, i: (b, h, i, 0, 0)),
            pl.BlockSpec((1, nh, SEQ, C_HEAD_DIM), lambda b, h, i: (b, h, 0, 0)),
            pl.BlockSpec((1, nh, SEQ // QC, C_HEAD_DIM, QC), lambda b, h, i: (b, h, 0, 0, 0)),
            pl.BlockSpec((1, SEQ, LANES), lambda b, h, i: (b, 0, 0)),
            pl.BlockSpec((1, nh, 1, tq), lambda b, h, i: (b, h, 0, i)),
        ],
        out_specs=pl.BlockSpec((1, tq, nh * C_HEAD_DIM), lambda b, h, i: (b, i, h)),
        out_shape=jax.ShapeDtypeStruct((bsz, SEQ, C_WIDTH), BF16),
        scratch_shapes=[
            pltpu.VMEM((nh, SEQ, QC), F32),
            pltpu.VMEM((nh, 1, tq), F32),
            pltpu.VMEM((nh, 1, tq), F32),
            pltpu.VMEM((nh, C_HEAD_DIM, tq), F32),
        ],
        compiler_params=_params(3),
        name="fox_attention",
    )(qt, k, vt, c, c_row)


def _proj_residual_kernel(x_ref, a_ref, w_ref, o_ref):
    o_ref[0] = x_ref[0] + _dot(a_ref[0], w_ref[...])


def _proj_residual(x, a, w, *, tm=512):
    bsz = x.shape[0]
    row_spec = pl.BlockSpec((1, tm, D_MODEL), lambda b, i: (b, i, 0))
    return pl.pallas_call(
        _proj_residual_kernel,
        grid=(bsz, SEQ // tm),
        in_specs=[row_spec, pl.BlockSpec((1, tm, a.shape[2]), lambda b, i: (b, i, 0)),
                  _resident(w.shape)],
        out_specs=row_spec,
        out_shape=jax.ShapeDtypeStruct(x.shape, F32),
        compiler_params=_params(2),
        name="proj_residual",
    )(x, a, w.astype(BF16))


def kernel(x, norm_mix_g, norm_ffn_g, final_norm_g, w_in_even, ln_v_g, ln_v_b, w_spatial,
           b_spatial, w_pool, pool_scale, w_out_even, w_in_odd, b_forget, w_out_odd, w_up,
           conv_w, conv_b, w_down):
    depth = norm_mix_g.shape[0]
    assert x.shape[1:] == (SEQ, D_MODEL)
    for i in range(depth):
        if i % 2 == 0:
            e = i // 2
            x = _even_mixer(x, norm_mix_g[i], w_in_even[e], ln_v_g[e], ln_v_b[e], w_spatial[e],
                            b_spatial[e], w_pool[e], pool_scale[e], w_out_even[e])
        else:
            o = i // 2
            qt, k, vt, c = _qkv_proj(x, norm_mix_g[i], w_in_odd[o], b_forget[o])
            att = _fox_attention(qt, k, vt, c)
            x = _proj_residual(x, att, w_out_odd[o])
        x = _conv_ffn(x, norm_ffn_g[i], w_up[i], conv_w[i], conv_b[i], w_down[i],
                      final_norm_g if i == depth - 1 else None)
    if depth == 0:
        raise NotImplementedError("depth 0 is not a configuration of this problem")
    return x
```

```python
import functools

import jax
import jax.numpy as jnp
from jax import lax
from jax.experimental import pallas as pl
from jax.experimental.pallas import tpu as pltpu

D_MODEL = 2048
SEQ = 2048
EPS = 1e-6
CHUNK = 64
A_BLOCK = 128
A_GROUPS = 4
A_WIDTH = D_MODEL // 2
A_GROUP_DIM = A_WIDTH // A_GROUPS
B_WINDOWS = (2, 4, 8, 16)
B_WIDTH = D_MODEL // 2
B_GROUP_DIM = B_WIDTH // len(B_WINDOWS)
C_HEADS = 16
C_HEAD_DIM = 128
C_WIDTH = C_HEADS * C_HEAD_DIM
D_FF = 5632
CONV_WIDTH = 3

LANES = 128
SUBLANES = 8
VMEM_LIMIT_BYTES = 60 * 1024 * 1024

POOL_HALO = 16
CONV_HALO = SUBLANES
FFN_ROWS = 256
FFN_COLS = 256
QC = 256

BF16 = jnp.bfloat16
F32 = jnp.float32
LOG2E = 1.4426950408889634


def _dot(a, b):
    return jnp.dot(a, b, preferred_element_type=F32)


def _rms_norm(x, g):
    return x * lax.rsqrt(jnp.mean(x * x, axis=-1, keepdims=True) + EPS) * g


def _gelu_exact(x):
    return 0.5 * x * (1.0 + lax.erf(x * (0.5 ** 0.5)))


def _resident(shape):
    zeros = (0,) * len(shape)
    return pl.BlockSpec(shape, lambda *_: zeros, pipeline_mode=pl.Buffered(1))


def _params(n_grid):
    return pltpu.CompilerParams(
        dimension_semantics=("arbitrary",) * n_grid,
        vmem_limit_bytes=VMEM_LIMIT_BYTES,
    )


def _even_mixer_kernel(x_ref, g_ref, win_ref, lng_ref, lnb_ref, ws_ref, bs_ref,
                       wpool_ref, pscale_ref, wout_ref, o_ref, pool_ext, mix_scr, *, tm):
    si = pl.program_id(1)
    x = x_ref[0]
    h = _rms_norm(x, g_ref[...]).astype(BF16)

    u = _gelu_exact(_dot(h, win_ref[:, 0:A_WIDTH]))
    v = _gelu_exact(_dot(h, win_ref[:, A_WIDTH:2 * A_WIDTH]))
    t_idx = lax.broadcasted_iota(jnp.int32, (A_BLOCK, A_BLOCK), 0) // CHUNK
    s_idx = lax.broadcasted_iota(jnp.int32, (A_BLOCK, A_BLOCK), 1) // CHUNK
    causal = t_idx >= s_idx
    for g in range(A_GROUPS):
        cols = slice(g * A_GROUP_DIM, (g + 1) * A_GROUP_DIM)
        vg = v[:, cols]
        mu = jnp.mean(vg, axis=-1, keepdims=True)
        var = jnp.mean(jnp.square(vg - mu), axis=-1, keepdims=True)
        vn = ((vg - mu) * lax.rsqrt(var + EPS) * lng_ref[:, cols] + lnb_ref[:, cols]).astype(BF16)
        w_g = jnp.where(causal, ws_ref[g], 0.0).astype(BF16)
        for n in range(tm // A_BLOCK):
            rows = slice(n * A_BLOCK, (n + 1) * A_BLOCK)
            gate = _dot(w_g, vn[rows]) + bs_ref[g]
            mix_scr[rows, cols] = (u[rows, cols] * gate).astype(BF16)

    @pl.when(si == 0)
    def _():
        pool_ext[0:POOL_HALO, :] = jnp.zeros((POOL_HALO, B_WIDTH), F32)

    @pl.when(si > 0)
    def _():
        pool_ext[0:POOL_HALO, :] = pool_ext[tm:tm + POOL_HALO, :]

    pool_ext[POOL_HALO:POOL_HALO + tm, :] = _dot(h, win_ref[:, 2 * A_WIDTH:2 * A_WIDTH + B_WIDTH])
    pos = si * tm + lax.broadcasted_iota(jnp.int32, (tm, 1), 0)
    for j, w in enumerate(B_WINDOWS):
        cols = slice(j * B_GROUP_DIM, (j + 1) * B_GROUP_DIM)
        xg = pool_ext[POOL_HALO:POOL_HALO + tm, cols]
        acc = xg
        for k in range(1, w):
            acc = acc + pool_ext[POOL_HALO - k:POOL_HALO - k + tm, cols]
        count = jnp.minimum(pos + 1, w).astype(F32)
        d = (acc / count - xg).astype(BF16)
        yb = _dot(d, wpool_ref[j]) * pscale_ref[:, cols]
        mix_scr[:, A_WIDTH + j * B_GROUP_DIM:A_WIDTH + (j + 1) * B_GROUP_DIM] = yb.astype(BF16)

    o_ref[0] = x + _dot(mix_scr[...], wout_ref[...])


def _even_mixer(x, g, w_in, ln_g, ln_b, w_s, b_s, w_pool, p_scale, w_out, *, tm=256):
    bsz = x.shape[0]
    even_in = 2 * A_WIDTH + B_WIDTH
    row_spec = pl.BlockSpec((1, tm, D_MODEL), lambda b, i: (b, i, 0))
    return pl.pallas_call(
        functools.partial(_even_mixer_kernel, tm=tm),
        grid=(bsz, SEQ // tm),
        in_specs=[
            row_spec,
            _resident((1, D_MODEL)),
            _resident((D_MODEL, even_in)),
            _resident((1, A_WIDTH)),
            _resident((1, A_WIDTH)),
            _resident((A_GROUPS, A_BLOCK, A_BLOCK)),
            _resident((A_GROUPS, A_BLOCK, 1)),
            _resident((len(B_WINDOWS), B_GROUP_DIM, B_GROUP_DIM)),
            _resident((1, B_WIDTH)),
            _resident((A_WIDTH + B_WIDTH, D_MODEL)),
        ],
        out_specs=row_spec,
        out_shape=jax.ShapeDtypeStruct(x.shape, F32),
        scratch_shapes=[
            pltpu.VMEM((tm + POOL_HALO, B_WIDTH), F32),
            pltpu.VMEM((tm, A_WIDTH + B_WIDTH), BF16),
        ],
        compiler_params=_params(2),
        name="even_mixer",
    )(x, g.reshape(1, D_MODEL), w_in.astype(BF16), ln_g.reshape(1, A_WIDTH),
      ln_b.reshape(1, A_WIDTH), w_s, b_s.reshape(A_GROUPS, A_BLOCK, 1),
      w_pool.astype(BF16), p_scale.reshape(1, B_WIDTH), w_out.astype(BF16))


def _conv_ffn_kernel(x_ref, g_ref, wg_ref, wv_ref, cwg_ref, cwv_ref, cbg_ref, cbv_ref,
                     wd_ref, *rest, tm, nf, final_norm):
    if final_norm:
        gfin_ref, o_ref, h_scr, ext_a, ext_b, act_scr, carry = rest
    else:
        o_ref, h_scr, ext_a, ext_b, act_scr, carry = rest
    ext = (ext_a, ext_b)
    b = pl.program_id(0)
    si = pl.program_id(1)
    f = pl.program_id(2)

    row_blocks = [slice(r, r + FFN_ROWS) for r in range(0, tm, FFN_ROWS)]

    tf = wg_ref.shape[1]
    n_pieces = 2 * tf // FFN_COLS
    act_cols = tf // n_pieces

    def up_piece(ext_w, rows, q):
        part, c0 = divmod(q * FFN_COLS, tf)
        cols = slice(c0, c0 + FFN_COLS)
        w_ref = (wg_ref, wv_ref)[part]
        up = _dot(h_scr[rows, :], w_ref[:, cols])
        if rows.start == 0:
            ext_w[part, 0:CONV_HALO, cols] = jnp.where(si > 0, carry[f, part, :, cols], 0.0)
        ext_w[part, CONV_HALO + rows.start:CONV_HALO + rows.stop, cols] = up
        if rows.stop == tm:
            carry[f, part, :, cols] = up[FFN_ROWS - CONV_HALO:FFN_ROWS, :]

    def act_piece(ext_r, rows, q):
        cols = slice(q * act_cols, (q + 1) * act_cols)

        def causal_conv(part, cw_ref, cb_ref):
            lo = rows.start
            e = ext_r[part, lo:lo + CONV_HALO + FFN_ROWS, cols]
            out = cb_ref[:, cols]
            for j in range(CONV_WIDTH):
                lag = CONV_WIDTH - 1 - j
                lagged = pltpu.roll(e, lag, axis=0) if lag else e
                out = out + lagged[CONV_HALO:, :] * cw_ref[j:j + 1, cols]
            return out

        cg = causal_conv(0, cwg_ref, cbg_ref)
        cv = causal_conv(1, cwv_ref, cbv_ref)
        return (cg * (1.0 / (1.0 + jnp.exp(-cg))) * cv).astype(BF16)

    def step(p, *, up, down, first=False, last=False):
        for rows in row_blocks:
            if first:
                x = x_ref[0, rows, :]
                h_scr[rows, :] = _rms_norm(x, g_ref[...]).astype(BF16)
                o_ref[0, rows, :] = x
            for q in range(n_pieces):
                if down:
                    act_scr[rows, q * act_cols:(q + 1) * act_cols] = act_piece(ext[1 - p], rows, q)
                if up:
                    up_piece(ext[p], rows, q)
            if down:
                out = o_ref[0, rows, :] + _dot(act_scr[rows, :], wd_ref[...])
                if last and final_norm:
                    out = _rms_norm(out, gfin_ref[...])
                o_ref[0, rows, :] = out

    @pl.when(f == 0)
    def _():
        @pl.when((b == 0) & (si == 0))
        def _():
            carry[...] = jnp.zeros(carry.shape, F32)

        step(0, up=True, down=False, first=True)

    for p in range(2):
        @pl.when((f >= 1) & (f < nf) & (f % 2 == p))
        def _():
            step(p, up=True, down=True)

    @pl.when(f == nf)
    def _():
        step(nf % 2, up=False, down=True, last=True)


def _conv_ffn(x, g, w_up, conv_w, conv_b, w_down, g_final=None, *, tm=512, tf=512):
    bsz = x.shape[0]
    nf = D_FF // tf
    final_norm = g_final is not None
    row_spec = pl.BlockSpec((1, tm, D_MODEL), lambda b, i, f: (b, i, 0))
    w_up = w_up.astype(BF16)
    conv_b = conv_b.reshape(1, 2 * D_FF)

    def chunk(f, lag):
        return jnp.clip(f - lag, 0, nf - 1)

    in_specs = [
        row_spec,
        pl.BlockSpec((1, D_MODEL), lambda b, i, f: (0, 0)),
        pl.BlockSpec((D_MODEL, tf), lambda b, i, f: (0, chunk(f, 0))),
        pl.BlockSpec((D_MODEL, tf), lambda b, i, f: (0, nf + chunk(f, 0))),
        pl.BlockSpec((CONV_WIDTH, tf), lambda b, i, f: (0, chunk(f, 1))),
        pl.BlockSpec((CONV_WIDTH, tf), lambda b, i, f: (0, nf + chunk(f, 1))),
        pl.BlockSpec((1, tf), lambda b, i, f: (0, chunk(f, 1))),
        pl.BlockSpec((1, tf), lambda b, i, f: (0, nf + chunk(f, 1))),
        pl.BlockSpec((tf, D_MODEL), lambda b, i, f: (chunk(f, 1), 0)),
    ]
    args = [x, g.reshape(1, D_MODEL), w_up, w_up, conv_w, conv_w, conv_b, conv_b,
            w_down.astype(BF16)]
    if final_norm:
        in_specs.append(pl.BlockSpec((1, D_MODEL), lambda b, i, f: (0, 0)))
        args.append(g_final.reshape(1, D_MODEL))
    return pl.pallas_call(
        functools.partial(_conv_ffn_kernel, tm=tm, nf=nf, final_norm=final_norm),
        grid=(bsz, SEQ // tm, nf + 1),
        in_specs=in_specs,
        out_specs=row_spec,
        out_shape=jax.ShapeDtypeStruct(x.shape, F32),
        scratch_shapes=[
            pltpu.VMEM((tm, D_MODEL), BF16),
            pltpu.VMEM((2, tm + CONV_HALO, tf), F32),
            pltpu.VMEM((2, tm + CONV_HALO, tf), F32),
            pltpu.VMEM((tm, tf), BF16),
            pltpu.VMEM((nf, 2, CONV_HALO, tf), F32),
        ],
        compiler_params=_params(3),
        name="conv_ffn_final" if final_norm else "conv_ffn",
    )(*args)


def _log_sigmoid(x):
    return jnp.minimum(x, 0.0) - jnp.log1p(jnp.exp(-jnp.abs(x)))


def _qkv_kernel(x_ref, g_ref, wqt_ref, wk_ref, wvt_ref, wf_ref, bf_ref,
                qt_ref, k_ref, vt_ref, c_ref, carry, *, tm):
    si = pl.program_id(1)
    hf = _rms_norm(x_ref[0], g_ref[...])
    h = hf.astype(BF16)
    ht = hf.T.astype(BF16)
    q_scale = C_HEAD_DIM ** -0.5 * LOG2E
    kk = _dot(h, wk_ref[...])
    qt = _dot(wqt_ref[...], ht) * q_scale
    vt = _dot(wvt_ref[...], ht)
    for hd in range(C_HEADS):
        dims = slice(hd * C_HEAD_DIM, (hd + 1) * C_HEAD_DIM)
        k_ref[0, hd] = kk[:, dims].astype(BF16)
        qt_ref[0, hd, 0] = qt[dims, :].astype(BF16)
        vt_ref[0, hd, 0] = vt[dims, :].astype(BF16)

    log_f = _log_sigmoid(_dot(h, wf_ref[...]) + bf_ref[...])
    row = lax.broadcasted_iota(jnp.int32, (tm, LANES), 0)
    k = 1
    while k < tm:
        log_f = log_f + jnp.where(row >= k, pltpu.roll(log_f, k, axis=0), 0.0)
        k *= 2

    @pl.when(si == 0)
    def _():
        carry[...] = jnp.zeros(carry.shape, F32)

    c = log_f + carry[0:1, :]
    c_ref[0] = c
    carry[...] = jnp.broadcast_to(c[tm - 1:tm, :], carry.shape)


def _qkv_proj(x, g, w_in, b_forget):
    bsz = x.shape[0]
    tm = QC
    wq_t = w_in[:, :C_WIDTH].T.astype(BF16)
    wk = w_in[:, C_WIDTH:2 * C_WIDTH].astype(BF16)
    wv_t = w_in[:, 2 * C_WIDTH:3 * C_WIDTH].T.astype(BF16)
    w_f = jnp.pad(w_in[:, 3 * C_WIDTH:], ((0, 0), (0, LANES - C_HEADS))).astype(BF16)
    b_f = jnp.pad(b_forget, (0, LANES - C_HEADS)).reshape(1, LANES)
    t_spec = pl.BlockSpec((1, C_HEADS, 1, C_HEAD_DIM, QC), lambda b, i: (b, 0, i, 0, 0))
    t_shape = jax.ShapeDtypeStruct((bsz, C_HEADS, SEQ // QC, C_HEAD_DIM, QC), BF16)
    return pl.pallas_call(
        functools.partial(_qkv_kernel, tm=tm),
        grid=(bsz, SEQ // tm),
        in_specs=[
            pl.BlockSpec((1, tm, D_MODEL), lambda b, i: (b, i, 0)),
            _resident((1, D_MODEL)),
            _resident((C_WIDTH, D_MODEL)),
            _resident((D_MODEL, C_WIDTH)),
            _resident((C_WIDTH, D_MODEL)),
            _resident((D_MODEL, LANES)),
            _resident((1, LANES)),
        ],
        out_specs=[t_spec,
                   pl.BlockSpec((1, C_HEADS, tm, C_HEAD_DIM), lambda b, i: (b, 0, i, 0)),
                   t_spec,
                   pl.BlockSpec((1, tm, LANES), lambda b, i: (b, i, 0))],
        out_shape=[t_shape,
                   jax.ShapeDtypeStruct((bsz, C_HEADS, SEQ, C_HEAD_DIM), BF16),
                   t_shape,
                   jax.ShapeDtypeStruct((bsz, SEQ, LANES), F32)],
        scratch_shapes=[pltpu.VMEM((SUBLANES, LANES), F32)],
        compiler_params=_params(2),
        name="qkv_proj",
    )(x, g.reshape(1, D_MODEL), wq_t, wk, wv_t, w_f, b_f)


def _fox_attention_kernel(qt_ref, k_ref, vt_ref, c_ref, crow_ref, o_ref,
                          ck_scr, m_scr, l_scr, acc_scr, *, tq, nh):
    hp = pl.program_id(1)
    qi = pl.program_id(2)
    nc = tq // QC

    @pl.when(qi == 0)
    def _():
        lane = lax.broadcasted_iota(jnp.int32, (SEQ, LANES), 1)
        c_all = c_ref[0] * LOG2E
        for j in range(nh):
            col = jnp.sum(jnp.where(lane == hp * nh + j, c_all, 0.0), axis=1, keepdims=True)
            ck_scr[j] = jnp.broadcast_to(col, (SEQ, QC))

    m_scr[...] = jnp.full(m_scr.shape, -jnp.inf, F32)
    l_scr[...] = jnp.zeros(l_scr.shape, F32)
    acc_scr[...] = jnp.zeros(acc_scr.shape, F32)

    def scores(j, c, kt):
        ks = pl.multiple_of(kt * QC, QC)
        return _dot(k_ref[0, j, pl.ds(ks, QC), :], qt_ref[0, j, c])

    def update(j, c, kt, diagonal, t):
        cs = slice(c * QC, (c + 1) * QC)
        ks = pl.multiple_of(kt * QC, QC)
        t = t - ck_scr[j, pl.ds(ks, QC), :]
        if diagonal:
            key = lax.broadcasted_iota(jnp.int32, (QC, QC), 0)
            qry = lax.broadcasted_iota(jnp.int32, (QC, QC), 1)
            t = jnp.where(key <= qry, t, -jnp.inf)
        c_q = crow_ref[0, j, :, cs] * LOG2E
        m_old = m_scr[j, :, cs]
        m_new = jnp.maximum(m_old, c_q + jnp.max(t, axis=0, keepdims=True))
        alpha = jnp.exp2(m_old - m_new)
        p = jnp.exp2(t + (c_q - m_new))
        l_scr[j, :, cs] = alpha * l_scr[j, :, cs] + jnp.sum(p, axis=0, keepdims=True)
        pv = _dot(vt_ref[0, j, kt], p.astype(BF16))
        acc_scr[j, :, cs] = alpha * acc_scr[j, :, cs] + pv
        m_scr[j, :, cs] = m_new

    def run(blocks):
        ahead = 4
        ts = [scores(*b[:3]) for b in blocks[:ahead]]
        for n, (j, c, kt, diagonal) in enumerate(blocks):
            if n + ahead < len(blocks):
                ts.append(scores(*blocks[n + ahead][:3]))
            update(j, c, kt, diagonal, ts[n])

    def full_tile(kj, carry):
        run([(j, c, kj * nc + s, False)
             for s in range(nc) for j in range(nh) for c in range(nc)])
        return carry

    lax.fori_loop(0, qi, full_tile, 0)
    run([(j, c, qi * nc + s, c == s)
         for s in range(nc) for j in range(nh) for c in range(s, nc)])
    for j in range(nh):
        o = acc_scr[j] / l_scr[j]
        o_ref[0, :, j * C_HEAD_DIM:(j + 1) * C_HEAD_DIM] = o.T.astype(BF16)


def _fox_attention(qt, k, vt, c, *, tq=512, nh=4):
    bsz = k.shape[0]
    nc = tq // QC
    c_row = jnp.transpose(c[:, :, :C_HEADS], (0, 2, 1)).reshape(bsz, C_HEADS, 1, SEQ)
    return pl.pallas_call(
        functools.partial(_fox_attention_kernel, tq=tq, nh=nh),
        grid=(bsz, C_HEADS // nh, SEQ // tq),
        in_specs=[
            pl.BlockSpec((1, nh, nc, C_HEAD_DIM, QC), lambda b, h, i: (b, h, i, 0, 0)),
            pl.BlockSpec((1, nh, SEQ, C_HEAD_DIM), lambda b, h, i: (b, h, 0, 0)),
            pl.BlockSpec((1, nh, SEQ // QC, C_HEAD_DIM, QC), lambda b, h, i: (b, h, 0, 0, 0)),
            pl.BlockSpec((1, SEQ, LANES), lambda b, h, i: (b, 0, 0)),
            pl.BlockSpec((1, nh, 1, tq), lambda b, h, i: (b, h, 0, i)),
        ],
        out_specs=pl.BlockSpec((1, tq, nh * C_HEAD_DIM), lambda b, h, i: (b, i, h)),
        out_shape=jax.ShapeDtypeStruct((bsz, SEQ, C_WIDTH), BF16),
        scratch_shapes=[
            pltpu.VMEM((nh, SEQ, QC), F32),
            pltpu.VMEM((nh, 1, tq), F32),
            pltpu.VMEM((nh, 1, tq), F32),
            pltpu.VMEM((nh, C_HEAD_DIM, tq), F32),
        ],
        compiler_params=_params(3),
        name="fox_attention",
    )(qt, k, vt, c, c_row)


def _proj_residual_kernel(x_ref, a_ref, w_ref, o_ref):
    o_ref[0] = x_ref[0] + _dot(a_ref[0], w_ref[...])


def _proj_residual(x, a, w, *, tm=512):
    bsz = x.shape[0]
    row_spec = pl.BlockSpec((1, tm, D_MODEL), lambda b, i: (b, i, 0))
    return pl.pallas_call(
        _proj_residual_kernel,
        grid=(bsz, SEQ // tm),
        in_specs=[row_spec, pl.BlockSpec((1, tm, a.shape[2]), lambda b, i: (b, i, 0)),
                  _resident(w.shape)],
        out_specs=row_spec,
        out_shape=jax.ShapeDtypeStruct(x.shape, F32),
        compiler_params=_params(2),
        name="proj_residual",
    )(x, a, w.astype(BF16))


def kernel(x, norm_mix_g, norm_ffn_g, final_norm_g, w_in_even, ln_v_g, ln_v_b, w_spatial,
           b_spatial, w_pool, pool_scale, w_out_even, w_in_odd, b_forget, w_out_odd, w_up,
           conv_w, conv_b, w_down):
    depth = norm_mix_g.shape[0]
    assert x.shape[1:] == (SEQ, D_MODEL)
    for i in range(depth):
        if i % 2 == 0:
            e = i // 2
            x = _even_mixer(x, norm_mix_g[i], w_in_even[e], ln_v_g[e], ln_v_b[e], w_spatial[e],
                            b_spatial[e], w_pool[e], pool_scale[e], w_out_even[e])
        else:
            o = i // 2
            qt, k, vt, c = _qkv_proj(x, norm_mix_g[i], w_in_odd[o], b_forget[o])
            att = _fox_attention(qt, k, vt, c)
            x = _proj_residual(x, att, w_out_odd[o])
        x = _conv_ffn(x, norm_ffn_g[i], w_up[i], conv_w[i], conv_b[i], w_down[i],
                      final_norm_g if i == depth - 1 else None)
    if depth == 0:
        raise NotImplementedError("depth 0 is not a configuration of this problem")
    return x
```

```python
import functools

import jax
import jax.numpy as jnp
from jax import lax
from jax.experimental import pallas as pl
from jax.experimental.pallas import tpu as pltpu

D_MODEL = 2048
SEQ = 2048
EPS = 1e-6
CHUNK = 64
A_BLOCK = 128
A_GROUPS = 4
A_WIDTH = D_MODEL // 2
A_GROUP_DIM = A_WIDTH // A_GROUPS
B_WINDOWS = (2, 4, 8, 16)
B_WIDTH = D_MODEL // 2
B_GROUP_DIM = B_WIDTH // len(B_WINDOWS)
C_HEADS = 16
C_HEAD_DIM = 128
C_WIDTH = C_HEADS * C_HEAD_DIM
D_FF = 5632
CONV_WIDTH = 3

LANES = 128
SUBLANES = 8
VMEM_LIMIT_BYTES = 60 * 1024 * 1024

POOL_HALO = 16
CONV_HALO = SUBLANES
FFN_ROWS = 256
FFN_COLS = 256
QC = 256

BF16 = jnp.bfloat16
F32 = jnp.float32
LOG2E = 1.4426950408889634


def _dot(a, b):
    return jnp.dot(a, b, preferred_element_type=F32)


def _rms_norm(x, g):
    return x * lax.rsqrt(jnp.mean(x * x, axis=-1, keepdims=True) + EPS) * g


def _gelu_exact(x):
    return 0.5 * x * (1.0 + lax.erf(x * (0.5 ** 0.5)))


def _resident(shape):
    zeros = (0,) * len(shape)
    return pl.BlockSpec(shape, lambda *_: zeros, pipeline_mode=pl.Buffered(1))


def _params(n_grid):
    return pltpu.CompilerParams(
        dimension_semantics=("arbitrary",) * n_grid,
        vmem_limit_bytes=VMEM_LIMIT_BYTES,
    )


def _even_mixer_kernel(x_ref, g_ref, win_ref, lng_ref, lnb_ref, ws_ref, bs_ref,
                       wpool_ref, pscale_ref, wout_ref, o_ref, pool_ext, mix_scr, *, tm):
    si = pl.program_id(1)
    x = x_ref[0]
    h = _rms_norm(x, g_ref[...]).astype(BF16)

    u = _gelu_exact(_dot(h, win_ref[:, 0:A_WIDTH]))
    v = _gelu_exact(_dot(h, win_ref[:, A_WIDTH:2 * A_WIDTH]))
    t_idx = lax.broadcasted_iota(jnp.int32, (A_BLOCK, A_BLOCK), 0) // CHUNK
    s_idx = lax.broadcasted_iota(jnp.int32, (A_BLOCK, A_BLOCK), 1) // CHUNK
    causal = t_idx >= s_idx
    for g in range(A_GROUPS):
        cols = slice(g * A_GROUP_DIM, (g + 1) * A_GROUP_DIM)
        vg = v[:, cols]
        mu = jnp.mean(vg, axis=-1, keepdims=True)
        var = jnp.mean(jnp.square(vg - mu), axis=-1, keepdims=True)
        vn = ((vg - mu) * lax.rsqrt(var + EPS) * lng_ref[:, cols] + lnb_ref[:, cols]).astype(BF16)
        w_g = jnp.where(causal, ws_ref[g], 0.0).astype(BF16)
        for n in range(tm // A_BLOCK):
            rows = slice(n * A_BLOCK, (n + 1) * A_BLOCK)
            gate = _dot(w_g, vn[rows]) + bs_ref[g]
            mix_scr[rows, cols] = (u[rows, cols] * gate).astype(BF16)

    @pl.when(si == 0)
    def _():
        pool_ext[0:POOL_HALO, :] = jnp.zeros((POOL_HALO, B_WIDTH), F32)

    @pl.when(si > 0)
    def _():
        pool_ext[0:POOL_HALO, :] = pool_ext[tm:tm + POOL_HALO, :]

    pool_ext[POOL_HALO:POOL_HALO + tm, :] = _dot(h, win_ref[:, 2 * A_WIDTH:2 * A_WIDTH + B_WIDTH])
    pos = si * tm + lax.broadcasted_iota(jnp.int32, (tm, 1), 0)
    for j, w in enumerate(B_WINDOWS):
        cols = slice(j * B_GROUP_DIM, (j + 1) * B_GROUP_DIM)
        xg = pool_ext[POOL_HALO:POOL_HALO + tm, cols]
        acc = xg
        for k in range(1, w):
            acc = acc + pool_ext[POOL_HALO - k:POOL_HALO - k + tm, cols]
        count = jnp.minimum(pos + 1, w).astype(F32)
        d = (acc / count - xg).astype(BF16)
        yb = _dot(d, wpool_ref[j]) * pscale_ref[:, cols]
        mix_scr[:, A_WIDTH + j * B_GROUP_DIM:A_WIDTH + (j + 1) * B_GROUP_DIM] = yb.astype(BF16)

    o_ref[0] = x + _dot(mix_scr[...], wout_ref[...])


def _even_mixer(x, g, w_in, ln_g, ln_b, w_s, b_s, w_pool, p_scale, w_out, *, tm=256):
    bsz = x.shape[0]
    even_in = 2 * A_WIDTH + B_WIDTH
    row_spec = pl.BlockSpec((1, tm, D_MODEL), lambda b, i: (b, i, 0))
    return pl.pallas_call(
        functools.partial(_even_mixer_kernel, tm=tm),
        grid=(bsz, SEQ // tm),
        in_specs=[
            row_spec,
            _resident((1, D_MODEL)),
            _resident((D_MODEL, even_in)),
            _resident((1, A_WIDTH)),
            _resident((1, A_WIDTH)),
            _resident((A_GROUPS, A_BLOCK, A_BLOCK)),
            _resident((A_GROUPS, A_BLOCK, 1)),
            _resident((len(B_WINDOWS), B_GROUP_DIM, B_GROUP_DIM)),
            _resident((1, B_WIDTH)),
            _resident((A_WIDTH + B_WIDTH, D_MODEL)),
        ],
        out_specs=row_spec,
        out_shape=jax.ShapeDtypeStruct(x.shape, F32),
        scratch_shapes=[
            pltpu.VMEM((tm + POOL_HALO, B_WIDTH), F32),
            pltpu.VMEM((tm, A_WIDTH + B_WIDTH), BF16),
        ],
        compiler_params=_params(2),
        name="even_mixer",
    )(x, g.reshape(1, D_MODEL), w_in.astype(BF16), ln_g.reshape(1, A_WIDTH),
      ln_b.reshape(1, A_WIDTH), w_s, b_s.reshape(A_GROUPS, A_BLOCK, 1),
      w_pool.astype(BF16), p_scale.reshape(1, B_WIDTH), w_out.astype(BF16))


def _conv_ffn_kernel(x_ref, g_ref, wu_ref, cwg_ref, cwv_ref, cbg_ref, cbv_ref,
                     wd_ref, *rest, tm, nf, final_norm):
    if final_norm:
        gfin_ref, o_ref, h_scr, ext_a, ext_b, act_scr, carry = rest
    else:
        o_ref, h_scr, ext_a, ext_b, act_scr, carry = rest
    ext = (ext_a, ext_b)
    b = pl.program_id(0)
    si = pl.program_id(1)
    f = pl.program_id(2)

    row_blocks = [slice(r, r + FFN_ROWS) for r in range(0, tm, FFN_ROWS)]

    tf = wu_ref.shape[2] // 2
    n_pieces = 2 * tf // FFN_COLS
    act_cols = tf // n_pieces

    def up_piece(ext_w, rows, q):
        part, c0 = divmod(q * FFN_COLS, tf)
        cols = slice(c0, c0 + FFN_COLS)
        w = wu_ref[0, :, q * FFN_COLS:(q + 1) * FFN_COLS]
        up = _dot(h_scr[rows, :], w)
        if rows.start == 0:
            ext_w[part, 0:CONV_HALO, cols] = jnp.where(si > 0, carry[f, part, :, cols], 0.0)
        ext_w[part, CONV_HALO + rows.start:CONV_HALO + rows.stop, cols] = up
        if rows.stop == tm:
            carry[f, part, :, cols] = up[FFN_ROWS - CONV_HALO:FFN_ROWS, :]

    def act_piece(ext_r, rows, q):
        cols = slice(q * act_cols, (q + 1) * act_cols)

        def causal_conv(part, cw_ref, cb_ref):
            lo = rows.start
            e = ext_r[part, lo:lo + CONV_HALO + FFN_ROWS, cols]
            out = cb_ref[:, cols]
            for j in range(CONV_WIDTH):
                lag = CONV_WIDTH - 1 - j
                lagged = pltpu.roll(e, lag, axis=0) if lag else e
                out = out + lagged[CONV_HALO:, :] * cw_ref[j:j + 1, cols]
            return out

        cg = causal_conv(0, cwg_ref, cbg_ref)
        cv = causal_conv(1, cwv_ref, cbv_ref)
        return (cg * (1.0 / (1.0 + jnp.exp(-cg))) * cv).astype(BF16)

    def step(p, *, up, down, first=False, last=False):
        for rows in row_blocks:
            if first:
                x = x_ref[0, rows, :]
                h_scr[rows, :] = _rms_norm(x, g_ref[...]).astype(BF16)
                o_ref[0, rows, :] = x
            for q in range(n_pieces):
                if down:
                    act_scr[rows, q * act_cols:(q + 1) * act_cols] = act_piece(ext[1 - p], rows, q)
                if up:
                    up_piece(ext[p], rows, q)
            if down:
                out = o_ref[0, rows, :] + _dot(act_scr[rows, :], wd_ref[...])
                if last and final_norm:
                    out = _rms_norm(out, gfin_ref[...])
                o_ref[0, rows, :] = out

    @pl.when(f == 0)
    def _():
        @pl.when((b == 0) & (si == 0))
        def _():
            carry[...] = jnp.zeros(carry.shape, F32)

        step(0, up=True, down=False, first=True)

    for p in range(2):
        @pl.when((f >= 1) & (f < nf) & (f % 2 == p))
        def _():
            step(p, up=True, down=True)

    @pl.when(f == nf)
    def _():
        step(nf % 2, up=False, down=True, last=True)


def _conv_ffn(x, g, w_up, conv_w, conv_b, w_down, g_final=None, *, tm=512, tf=512):
    bsz = x.shape[0]
    nf = D_FF // tf
    final_norm = g_final is not None
    row_spec = pl.BlockSpec((1, tm, D_MODEL), lambda b, i, f: (b, i, 0))
    w_up = (w_up.astype(BF16).reshape(D_MODEL, 2, nf, tf).transpose(2, 0, 1, 3)
            .reshape(nf, D_MODEL, 2 * tf))
    conv_b = conv_b.reshape(1, 2 * D_FF)

    def chunk(f, lag):
        return jnp.clip(f - lag, 0, nf - 1)

    in_specs = [
        row_spec,
        pl.BlockSpec((1, D_MODEL), lambda b, i, f: (0, 0)),
        pl.BlockSpec((1, D_MODEL, 2 * tf), lambda b, i, f: (chunk(f, 0), 0, 0)),
        pl.BlockSpec((CONV_WIDTH, tf), lambda b, i, f: (0, chunk(f, 1))),
        pl.BlockSpec((CONV_WIDTH, tf), lambda b, i, f: (0, nf + chunk(f, 1))),
        pl.BlockSpec((1, tf), lambda b, i, f: (0, chunk(f, 1))),
        pl.BlockSpec((1, tf), lambda b, i, f: (0, nf + chunk(f, 1))),
        pl.BlockSpec((tf, D_MODEL), lambda b, i, f: (chunk(f, 1), 0)),
    ]
    args = [x, g.reshape(1, D_MODEL), w_up, conv_w, conv_w, conv_b, conv_b,
            w_down.astype(BF16)]
    if final_norm:
        in_specs.append(pl.BlockSpec((1, D_MODEL), lambda b, i, f: (0, 0)))
        args.append(g_final.reshape(1, D_MODEL))
    return pl.pallas_call(
        functools.partial(_conv_ffn_kernel, tm=tm, nf=nf, final_norm=final_norm),
        grid=(bsz, SEQ // tm, nf + 1),
        in_specs=in_specs,
        out_specs=row_spec,
        out_shape=jax.ShapeDtypeStruct(x.shape, F32),
        scratch_shapes=[
            pltpu.VMEM((tm, D_MODEL), BF16),
            pltpu.VMEM((2, tm + CONV_HALO, tf), F32),
            pltpu.VMEM((2, tm + CONV_HALO, tf), F32),
            pltpu.VMEM((tm, tf), BF16),
            pltpu.VMEM((nf, 2, CONV_HALO, tf), F32),
        ],
        compiler_params=_params(3),
        name="conv_ffn_final" if final_norm else "conv_ffn",
    )(*args)


def _log_sigmoid(x):
    return jnp.minimum(x, 0.0) - jnp.log1p(jnp.exp(-jnp.abs(x)))


def _qkv_kernel(x_ref, g_ref, wqt_ref, wk_ref, wvt_ref, wf_ref, bf_ref,
                qt_ref, k_ref, vt_ref, c_ref, carry, *, tm):
    si = pl.program_id(1)
    hf = _rms_norm(x_ref[0], g_ref[...])
    h = hf.astype(BF16)
    ht = hf.T.astype(BF16)
    q_scale = C_HEAD_DIM ** -0.5 * LOG2E
    kk = _dot(h, wk_ref[...])
    qt = _dot(wqt_ref[...], ht) * q_scale
    vt = _dot(wvt_ref[...], ht)
    for hd in range(C_HEADS):
        dims = slice(hd * C_HEAD_DIM, (hd + 1) * C_HEAD_DIM)
        k_ref[0, hd] = kk[:, dims].astype(BF16)
        qt_ref[0, hd, 0] = qt[dims, :].astype(BF16)
        vt_ref[0, hd, 0] = vt[dims, :].astype(BF16)

    log_f = _log_sigmoid(_dot(h, wf_ref[...]) + bf_ref[...])
    row = lax.broadcasted_iota(jnp.int32, (tm, LANES), 0)
    k = 1
    while k < tm:
        log_f = log_f + jnp.where(row >= k, pltpu.roll(log_f, k, axis=0), 0.0)
        k *= 2

    @pl.when(si == 0)
    def _():
        carry[...] = jnp.zeros(carry.shape, F32)

    c = log_f + carry[0:1, :]
    c_ref[0] = c
    carry[...] = jnp.broadcast_to(c[tm - 1:tm, :], carry.shape)


def _qkv_proj(x, g, w_in, b_forget):
    bsz = x.shape[0]
    tm = QC
    wq_t = w_in[:, :C_WIDTH].T.astype(BF16)
    wk = w_in[:, C_WIDTH:2 * C_WIDTH].astype(BF16)
    wv_t = w_in[:, 2 * C_WIDTH:3 * C_WIDTH].T.astype(BF16)
    w_f = jnp.pad(w_in[:, 3 * C_WIDTH:], ((0, 0), (0, LANES - C_HEADS))).astype(BF16)
    b_f = jnp.pad(b_forget, (0, LANES - C_HEADS)).reshape(1, LANES)
    t_spec = pl.BlockSpec((1, C_HEADS, 1, C_HEAD_DIM, QC), lambda b, i: (b, 0, i, 0, 0))
    t_shape = jax.ShapeDtypeStruct((bsz, C_HEADS, SEQ // QC, C_HEAD_DIM, QC), BF16)
    return pl.pallas_call(
        functools.partial(_qkv_kernel, tm=tm),
        grid=(bsz, SEQ // tm),
        in_specs=[
            pl.BlockSpec((1, tm, D_MODEL), lambda b, i: (b, i, 0)),
            _resident((1, D_MODEL)),
            _resident((C_WIDTH, D_MODEL)),
            _resident((D_MODEL, C_WIDTH)),
            _resident((C_WIDTH, D_MODEL)),
            _resident((D_MODEL, LANES)),
            _resident((1, LANES)),
        ],
        out_specs=[t_spec,
                   pl.BlockSpec((1, C_HEADS, tm, C_HEAD_DIM), lambda b, i: (b, 0, i, 0)),
                   t_spec,
                   pl.BlockSpec((1, tm, LANES), lambda b, i: (b, i, 0))],
        out_shape=[t_shape,
                   jax.ShapeDtypeStruct((bsz, C_HEADS, SEQ, C_HEAD_DIM), BF16),
                   t_shape,
                   jax.ShapeDtypeStruct((bsz, SEQ, LANES), F32)],
        scratch_shapes=[pltpu.VMEM((SUBLANES, LANES), F32)],
        compiler_params=_params(2),
        name="qkv_proj",
    )(x, g.reshape(1, D_MODEL), wq_t, wk, wv_t, w_f, b_f)


def _fox_attention_kernel(qt_ref, k_ref, vt_ref, c_ref, crow_ref, o_ref,
                          ck_scr, m_scr, l_scr, acc_scr, *, tq, nh):
    hp = pl.program_id(1)
    qi = pl.program_id(2)
    nc = tq // QC

    @pl.when(qi == 0)
    def _():
        lane = lax.broadcasted_iota(jnp.int32, (SEQ, LANES), 1)
        c_all = c_ref[0] * LOG2E
        for j in range(nh):
            col = jnp.sum(jnp.where(lane == hp * nh + j, c_all, 0.0), axis=1, keepdims=True)
            ck_scr[j] = jnp.broadcast_to(col, (SEQ, QC))

    m_scr[...] = jnp.full(m_scr.shape, -jnp.inf, F32)
    l_scr[...] = jnp.zeros(l_scr.shape, F32)
    acc_scr[...] = jnp.zeros(acc_scr.shape, F32)

    def scores(j, c, kt):
        ks = pl.multiple_of(kt * QC, QC)
        return _dot(k_ref[0, j, pl.ds(ks, QC), :], qt_ref[0, j, c])

    def update(j, c, kt, diagonal, t):
        cs = slice(c * QC, (c + 1) * QC)
        ks = pl.multiple_of(kt * QC, QC)
        t = t - ck_scr[j, pl.ds(ks, QC), :]
        if diagonal:
            key = lax.broadcasted_iota(jnp.int32, (QC, QC), 0)
            qry = lax.broadcasted_iota(jnp.int32, (QC, QC), 1)
            t = jnp.where(key <= qry, t, -jnp.inf)
        c_q = crow_ref[0, j, :, cs] * LOG2E
        m_old = m_scr[j, :, cs]
        m_new = jnp.maximum(m_old, c_q + jnp.max(t, axis=0, keepdims=True))
        alpha = jnp.exp2(m_old - m_new)
        p = jnp.exp2(t + (c_q - m_new))
        l_scr[j, :, cs] = alpha * l_scr[j, :, cs] + jnp.sum(p, axis=0, keepdims=True)
        pv = _dot(vt_ref[0, j, kt], p.astype(BF16))
        acc_scr[j, :, cs] = alpha * acc_scr[j, :, cs] + pv
        m_scr[j, :, cs] = m_new

    def run(blocks):
        ahead = 4
        ts = [scores(*b[:3]) for b in blocks[:ahead]]
        for n, (j, c, kt, diagonal) in enumerate(blocks):
            if n + ahead < len(blocks):
                ts.append(scores(*blocks[n + ahead][:3]))
            update(j, c, kt, diagonal, ts[n])

    def full_tile(kj, carry):
        run([(j, c, kj * nc + s, False)
             for s in range(nc) for j in range(nh) for c in range(nc)])
        return carry

    lax.fori_loop(0, qi, full_tile, 0)
    run([(j, c, qi * nc + s, c == s)
         for s in range(nc) for j in range(nh) for c in range(s, nc)])
    for j in range(nh):
        o = acc_scr[j] / l_scr[j]
        o_ref[0, :, j * C_HEAD_DIM:(j + 1) * C_HEAD_DIM] = o.T.astype(BF16)


def _fox_attention(qt, k, vt, c, *, tq=512, nh=4):
    bsz = k.shape[0]
    nc = tq // QC
    c_row = jnp.transpose(c[:, :, :C_HEADS], (0, 2, 1)).reshape(bsz, C_HEADS, 1, SEQ)
    return pl.pallas_call(
        functools.partial(_fox_attention_kernel, tq=tq, nh=nh),
        grid=(bsz, C_HEADS // nh, SEQ // tq),
        in_specs=[
            pl.BlockSpec((1, nh, nc, C_HEAD_DIM, QC), lambda b, h, i: (b, h, i, 0, 0)),
            pl.BlockSpec((1, nh, SEQ, C_HEAD_DIM), lambda b, h, i: (b, h, 0, 0)),
            pl.BlockSpec((1, nh, SEQ // QC, C_HEAD_DIM, QC), lambda b, h, i: (b, h, 0, 0, 0)),
            pl.BlockSpec((1, SEQ, LANES), lambda b, h, i: (b, 0, 0)),
            pl.BlockSpec((1, nh, 1, tq), lambda b, h, i: (b, h, 0, i)),
        ],
        out_specs=pl.BlockSpec((1, tq, nh * C_HEAD_DIM), lambda b, h, i: (b, i, h)),
        out_shape=jax.ShapeDtypeStruct((bsz, SEQ, C_WIDTH), BF16),
        scratch_shapes=[
            pltpu.VMEM((nh, SEQ, QC), F32),
            pltpu.VMEM((nh, 1, tq), F32),
            pltpu.VMEM((nh, 1, tq), F32),
            pltpu.VMEM((nh, C_HEAD_DIM, tq), F32),
        ],
        compiler_params=_params(3),
        name="fox_attention",
    )(qt, k, vt, c, c_row)


def _proj_residual_kernel(x_ref, a_ref, w_ref, o_ref):
    o_ref[0] = x_ref[0] + _dot(a_ref[0], w_ref[...])


def _proj_residual(x, a, w, *, tm=512):
    bsz = x.shape[0]
    row_spec = pl.BlockSpec((1, tm, D_MODEL), lambda b, i: (b, i, 0))
    return pl.pallas_call(
        _proj_residual_kernel,
        grid=(bsz, SEQ // tm),
        in_specs=[row_spec, pl.BlockSpec((1, tm, a.shape[2]), lambda b, i: (b, i, 0)),
                  _resident(w.shape)],
        out_specs=row_spec,
        out_shape=jax.ShapeDtypeStruct(x.shape, F32),
        compiler_params=_params(2),
        name="proj_residual",
    )(x, a, w.astype(BF16))


def kernel(x, norm_mix_g, norm_ffn_g, final_norm_g, w_in_even, ln_v_g, ln_v_b, w_spatial,
           b_spatial, w_pool, pool_scale, w_out_even, w_in_odd, b_forget, w_out_odd, w_up,
           conv_w, conv_b, w_down):
    depth = norm_mix_g.shape[0]
    assert x.shape[1:] == (SEQ, D_MODEL)
    for i in range(depth):
        if i % 2 == 0:
            e = i // 2
            x = _even_mixer(x, norm_mix_g[i], w_in_even[e], ln_v_g[e], ln_v_b[e], w_spatial[e],
                            b_spatial[e], w_pool[e], pool_scale[e], w_out_even[e])
        else:
            o = i // 2
            qt, k, vt, c = _qkv_proj(x, norm_mix_g[i], w_in_odd[o], b_forget[o])
            att = _fox_attention(qt, k, vt, c)
            x = _proj_residual(x, att, w_out_odd[o])
        x = _conv_ffn(x, norm_ffn_g[i], w_up[i], conv_w[i], conv_b[i], w_down[i],
                      final_norm_g if i == depth - 1 else None)
    if depth == 0:
        raise NotImplementedError("depth 0 is not a configuration of this problem")
    return x
```

```python
import functools

import jax
import jax.numpy as jnp
from jax import lax
from jax.experimental import pallas as pl
from jax.experimental.pallas import tpu as pltpu

D_MODEL = 2048
SEQ = 2048
EPS = 1e-6
CHUNK = 64
A_BLOCK = 128
A_GROUPS = 4
A_WIDTH = D_MODEL // 2
A_GROUP_DIM = A_WIDTH // A_GROUPS
B_WINDOWS = (2, 4, 8, 16)
B_WIDTH = D_MODEL // 2
B_GROUP_DIM = B_WIDTH // len(B_WINDOWS)
C_HEADS = 16
C_HEAD_DIM = 128
C_WIDTH = C_HEADS * C_HEAD_DIM
D_FF = 5632
CONV_WIDTH = 3

LANES = 128
SUBLANES = 8
VMEM_LIMIT_BYTES = 60 * 1024 * 1024

POOL_HALO = 16
CONV_HALO = SUBLANES
FFN_ROWS = 256
FFN_COLS = 256
QC = 256

BF16 = jnp.bfloat16
F32 = jnp.float32
LOG2E = 1.4426950408889634


def _dot(a, b):
    return jnp.dot(a, b, preferred_element_type=F32)


def _rms_norm(x, g):
    return x * lax.rsqrt(jnp.mean(x * x, axis=-1, keepdims=True) + EPS) * g


def _gelu_exact(x):
    return 0.5 * x * (1.0 + lax.erf(x * (0.5 ** 0.5)))


def _resident(shape):
    zeros = (0,) * len(shape)
    return pl.BlockSpec(shape, lambda *_: zeros, pipeline_mode=pl.Buffered(1))


def _params(n_grid):
    return pltpu.CompilerParams(
        dimension_semantics=("arbitrary",) * n_grid,
        vmem_limit_bytes=VMEM_LIMIT_BYTES,
    )


def _even_mixer_kernel(x_ref, g_ref, win_ref, lng_ref, lnb_ref, ws_ref, bs_ref,
                       wpool_ref, pscale_ref, wout_ref, o_ref, pool_ext, mix_scr, *, tm):
    si = pl.program_id(1)
    x = x_ref[0]
    h = _rms_norm(x, g_ref[...]).astype(BF16)

    u = _gelu_exact(_dot(h, win_ref[:, 0:A_WIDTH]))
    v = _gelu_exact(_dot(h, win_ref[:, A_WIDTH:2 * A_WIDTH]))
    t_idx = lax.broadcasted_iota(jnp.int32, (A_BLOCK, A_BLOCK), 0) // CHUNK
    s_idx = lax.broadcasted_iota(jnp.int32, (A_BLOCK, A_BLOCK), 1) // CHUNK
    causal = t_idx >= s_idx
    for g in range(A_GROUPS):
        cols = slice(g * A_GROUP_DIM, (g + 1) * A_GROUP_DIM)
        vg = v[:, cols]
        mu = jnp.mean(vg, axis=-1, keepdims=True)
        var = jnp.mean(jnp.square(vg - mu), axis=-1, keepdims=True)
        vn = ((vg - mu) * lax.rsqrt(var + EPS) * lng_ref[:, cols] + lnb_ref[:, cols]).astype(BF16)
        w_g = jnp.where(causal, ws_ref[g], 0.0).astype(BF16)
        for n in range(tm // A_BLOCK):
            rows = slice(n * A_BLOCK, (n + 1) * A_BLOCK)
            gate = _dot(w_g, vn[rows]) + bs_ref[g]
            mix_scr[rows, cols] = (u[rows, cols] * gate).astype(BF16)

    @pl.when(si == 0)
    def _():
        pool_ext[0:POOL_HALO, :] = jnp.zeros((POOL_HALO, B_WIDTH), F32)

    @pl.when(si > 0)
    def _():
        pool_ext[0:POOL_HALO, :] = pool_ext[tm:tm + POOL_HALO, :]

    pool_ext[POOL_HALO:POOL_HALO + tm, :] = _dot(h, win_ref[:, 2 * A_WIDTH:2 * A_WIDTH + B_WIDTH])
    pos = si * tm + lax.broadcasted_iota(jnp.int32, (tm, 1), 0)
    for j, w in enumerate(B_WINDOWS):
        cols = slice(j * B_GROUP_DIM, (j + 1) * B_GROUP_DIM)
        xg = pool_ext[POOL_HALO:POOL_HALO + tm, cols]
        acc = xg
        for k in range(1, w):
            acc = acc + pool_ext[POOL_HALO - k:POOL_HALO - k + tm, cols]
        count = jnp.minimum(pos + 1, w).astype(F32)
        d = (acc / count - xg).astype(BF16)
        yb = _dot(d, wpool_ref[j]) * pscale_ref[:, cols]
        mix_scr[:, A_WIDTH + j * B_GROUP_DIM:A_WIDTH + (j + 1) * B_GROUP_DIM] = yb.astype(BF16)

    o_ref[0] = x + _dot(mix_scr[...], wout_ref[...])


def _even_mixer(x, g, w_in, ln_g, ln_b, w_s, b_s, w_pool, p_scale, w_out, *, tm=256):
    bsz = x.shape[0]
    even_in = 2 * A_WIDTH + B_WIDTH
    row_spec = pl.BlockSpec((1, tm, D_MODEL), lambda b, i: (b, i, 0))
    return pl.pallas_call(
        functools.partial(_even_mixer_kernel, tm=tm),
        grid=(bsz, SEQ // tm),
        in_specs=[
            row_spec,
            _resident((1, D_MODEL)),
            _resident((D_MODEL, even_in)),
            _resident((1, A_WIDTH)),
            _resident((1, A_WIDTH)),
            _resident((A_GROUPS, A_BLOCK, A_BLOCK)),
            _resident((A_GROUPS, A_BLOCK, 1)),
            _resident((len(B_WINDOWS), B_GROUP_DIM, B_GROUP_DIM)),
            _resident((1, B_WIDTH)),
            _resident((A_WIDTH + B_WIDTH, D_MODEL)),
        ],
        out_specs=row_spec,
        out_shape=jax.ShapeDtypeStruct(x.shape, F32),
        scratch_shapes=[
            pltpu.VMEM((tm + POOL_HALO, B_WIDTH), F32),
            pltpu.VMEM((tm, A_WIDTH + B_WIDTH), BF16),
        ],
        compiler_params=_params(2),
        name="even_mixer",
    )(x, g.reshape(1, D_MODEL), w_in.astype(BF16), ln_g.reshape(1, A_WIDTH),
      ln_b.reshape(1, A_WIDTH), w_s, b_s.reshape(A_GROUPS, A_BLOCK, 1),
      w_pool.astype(BF16), p_scale.reshape(1, B_WIDTH), w_out.astype(BF16))


def _conv_ffn_kernel(x_ref, g_ref, wg_ref, wv_ref, cwg_ref, cwv_ref, cbg_ref, cbv_ref,
                     wd_ref, *rest, tm, nf, final_norm):
    if final_norm:
        gfin_ref, o_ref, h_scr, ext_a, ext_b, act_scr, carry = rest
    else:
        o_ref, h_scr, ext_a, ext_b, act_scr, carry = rest
    ext = (ext_a, ext_b)
    b = pl.program_id(0)
    si = pl.program_id(1)
    f = pl.program_id(2)

    row_blocks = [slice(r, r + FFN_ROWS) for r in range(0, tm, FFN_ROWS)]

    tf = wg_ref.shape[1]
    n_pieces = 2 * tf // FFN_COLS
    act_cols = tf // n_pieces

    def up_piece(ext_w, rows, q):
        part, c0 = divmod(q * FFN_COLS, tf)
        cols = slice(c0, c0 + FFN_COLS)
        w_ref = (wg_ref, wv_ref)[part]
        up = _dot(h_scr[rows, :], w_ref[:, cols])
        if rows.start == 0:
            ext_w[part, 0:CONV_HALO, cols] = jnp.where(si > 0, carry[f, part, :, cols], 0.0)
        ext_w[part, CONV_HALO + rows.start:CONV_HALO + rows.stop, cols] = up
        if rows.stop == tm:
            carry[f, part, :, cols] = up[FFN_ROWS - CONV_HALO:FFN_ROWS, :]

    def act_piece(ext_r, rows, q):
        cols = slice(q * act_cols, (q + 1) * act_cols)

        def causal_conv(part, cw_ref, cb_ref):
            lo = rows.start
            e = ext_r[part, lo:lo + CONV_HALO + FFN_ROWS, cols]
            out = cb_ref[:, cols]
            for j in range(CONV_WIDTH):
                lag = CONV_WIDTH - 1 - j
                lagged = pltpu.roll(e, lag, axis=0) if lag else e
                out = out + lagged[CONV_HALO:, :] * cw_ref[j:j + 1, cols]
            return out

        cg = causal_conv(0, cwg_ref, cbg_ref)
        cv = causal_conv(1, cwv_ref, cbv_ref)
        return (cg * (1.0 / (1.0 + jnp.exp(-cg))) * cv).astype(BF16)

    def step(p, *, up, down, first=False, last=False):
        for rows in row_blocks:
            if first:
                x = x_ref[0, rows, :]
                h_scr[rows, :] = _rms_norm(x, g_ref[...]).astype(BF16)
                o_ref[0, rows, :] = x
            for q in range(n_pieces):
                if down:
                    act_scr[rows, q * act_cols:(q + 1) * act_cols] = act_piece(ext[1 - p], rows, q)
                if up:
                    up_piece(ext[p], rows, q)
            if down:
                out = o_ref[0, rows, :] + _dot(act_scr[rows, :], wd_ref[...])
                if last and final_norm:
                    out = _rms_norm(out, gfin_ref[...])
                o_ref[0, rows, :] = out

    @pl.when(f == 0)
    def _():
        @pl.when((b == 0) & (si == 0))
        def _():
            carry[...] = jnp.zeros(carry.shape, F32)

        step(0, up=True, down=False, first=True)

    for p in range(2):
        @pl.when((f >= 1) & (f < nf) & (f % 2 == p))
        def _():
            step(p, up=True, down=True)

    @pl.when(f == nf)
    def _():
        step(nf % 2, up=False, down=True, last=True)


def _conv_ffn(x, g, w_up, conv_w, conv_b, w_down, g_final=None, *, tm=1024, tf=256):
    bsz = x.shape[0]
    nf = D_FF // tf
    final_norm = g_final is not None
    row_spec = pl.BlockSpec((1, tm, D_MODEL), lambda b, i, f: (b, i, 0))
    w_up = w_up.astype(BF16)
    conv_b = conv_b.reshape(1, 2 * D_FF)

    def chunk(f, lag):
        return jnp.clip(f - lag, 0, nf - 1)

    in_specs = [
        row_spec,
        pl.BlockSpec((1, D_MODEL), lambda b, i, f: (0, 0)),
        pl.BlockSpec((D_MODEL, tf), lambda b, i, f: (0, chunk(f, 0))),
        pl.BlockSpec((D_MODEL, tf), lambda b, i, f: (0, nf + chunk(f, 0))),
        pl.BlockSpec((CONV_WIDTH, tf), lambda b, i, f: (0, chunk(f, 1))),
        pl.BlockSpec((CONV_WIDTH, tf), lambda b, i, f: (0, nf + chunk(f, 1))),
        pl.BlockSpec((1, tf), lambda b, i, f: (0, chunk(f, 1))),
        pl.BlockSpec((1, tf), lambda b, i, f: (0, nf + chunk(f, 1))),
        pl.BlockSpec((tf, D_MODEL), lambda b, i, f: (chunk(f, 1), 0)),
    ]
    args = [x, g.reshape(1, D_MODEL), w_up, w_up, conv_w, conv_w, conv_b, conv_b,
            w_down.astype(BF16)]
    if final_norm:
        in_specs.append(pl.BlockSpec((1, D_MODEL), lambda b, i, f: (0, 0)))
        args.append(g_final.reshape(1, D_MODEL))
    return pl.pallas_call(
        functools.partial(_conv_ffn_kernel, tm=tm, nf=nf, final_norm=final_norm),
        grid=(bsz, SEQ // tm, nf + 1),
        in_specs=in_specs,
        out_specs=row_spec,
        out_shape=jax.ShapeDtypeStruct(x.shape, F32),
        scratch_shapes=[
            pltpu.VMEM((tm, D_MODEL), BF16),
            pltpu.VMEM((2, tm + CONV_HALO, tf), F32),
            pltpu.VMEM((2, tm + CONV_HALO, tf), F32),
            pltpu.VMEM((tm, tf), BF16),
            pltpu.VMEM((nf, 2, CONV_HALO, tf), F32),
        ],
        compiler_params=_params(3),
        name="conv_ffn_final" if final_norm else "conv_ffn",
    )(*args)


def _log_sigmoid(x):
    return jnp.minimum(x, 0.0) - jnp.log1p(jnp.exp(-jnp.abs(x)))


def _qkv_kernel(x_ref, g_ref, wqt_ref, wk_ref, wvt_ref, wf_ref, bf_ref,
                qt_ref, k_ref, vt_ref, c_ref, carry, *, tm):
    si = pl.program_id(1)
    hf = _rms_norm(x_ref[0], g_ref[...])
    h = hf.astype(BF16)
    ht = hf.T.astype(BF16)
    q_scale = C_HEAD_DIM ** -0.5 * LOG2E
    kk = _dot(h, wk_ref[...])
    qt = _dot(wqt_ref[...], ht) * q_scale
    vt = _dot(wvt_ref[...], ht)
    for hd in range(C_HEADS):
        dims = slice(hd * C_HEAD_DIM, (hd + 1) * C_HEAD_DIM)
        k_ref[0, hd] = kk[:, dims].astype(BF16)
        qt_ref[0, hd, 0] = qt[dims, :].astype(BF16)
        vt_ref[0, hd, 0] = vt[dims, :].astype(BF16)

    log_f = _log_sigmoid(_dot(h, wf_ref[...]) + bf_ref[...])
    row = lax.broadcasted_iota(jnp.int32, (tm, LANES), 0)
    k = 1
    while k < tm:
        log_f = log_f + jnp.where(row >= k, pltpu.roll(log_f, k, axis=0), 0.0)
        k *= 2

    @pl.when(si == 0)
    def _():
        carry[...] = jnp.zeros(carry.shape, F32)

    c = log_f + carry[0:1, :]
    c_ref[0] = c
    carry[...] = jnp.broadcast_to(c[tm - 1:tm, :], carry.shape)


def _qkv_proj(x, g, w_in, b_forget):
    bsz = x.shape[0]
    tm = QC
    wq_t = w_in[:, :C_WIDTH].T.astype(BF16)
    wk = w_in[:, C_WIDTH:2 * C_WIDTH].astype(BF16)
    wv_t = w_in[:, 2 * C_WIDTH:3 * C_WIDTH].T.astype(BF16)
    w_f = jnp.pad(w_in[:, 3 * C_WIDTH:], ((0, 0), (0, LANES - C_HEADS))).astype(BF16)
    b_f = jnp.pad(b_forget, (0, LANES - C_HEADS)).reshape(1, LANES)
    t_spec = pl.BlockSpec((1, C_HEADS, 1, C_HEAD_DIM, QC), lambda b, i: (b, 0, i, 0, 0))
    t_shape = jax.ShapeDtypeStruct((bsz, C_HEADS, SEQ // QC, C_HEAD_DIM, QC), BF16)
    return pl.pallas_call(
        functools.partial(_qkv_kernel, tm=tm),
        grid=(bsz, SEQ // tm),
        in_specs=[
            pl.BlockSpec((1, tm, D_MODEL), lambda b, i: (b, i, 0)),
            _resident((1, D_MODEL)),
            _resident((C_WIDTH, D_MODEL)),
            _resident((D_MODEL, C_WIDTH)),
            _resident((C_WIDTH, D_MODEL)),
            _resident((D_MODEL, LANES)),
            _resident((1, LANES)),
        ],
        out_specs=[t_spec,
                   pl.BlockSpec((1, C_HEADS, tm, C_HEAD_DIM), lambda b, i: (b, 0, i, 0)),
                   t_spec,
                   pl.BlockSpec((1, tm, LANES), lambda b, i: (b, i, 0))],
        out_shape=[t_shape,
                   jax.ShapeDtypeStruct((bsz, C_HEADS, SEQ, C_HEAD_DIM), BF16),
                   t_shape,
                   jax.ShapeDtypeStruct((bsz, SEQ, LANES), F32)],
        scratch_shapes=[pltpu.VMEM((SUBLANES, LANES), F32)],
        compiler_params=_params(2),
        name="qkv_proj",
    )(x, g.reshape(1, D_MODEL), wq_t, wk, wv_t, w_f, b_f)


def _fox_attention_kernel(qt_ref, k_ref, vt_ref, c_ref, crow_ref, o_ref,
                          ck_scr, m_scr, l_scr, acc_scr, *, tq, nh):
    hp = pl.program_id(1)
    qi = pl.program_id(2)
    nc = tq // QC

    @pl.when(qi == 0)
    def _():
        lane = lax.broadcasted_iota(jnp.int32, (SEQ, LANES), 1)
        c_all = c_ref[0] * LOG2E
        for j in range(nh):
            col = jnp.sum(jnp.where(lane == hp * nh + j, c_all, 0.0), axis=1, keepdims=True)
            ck_scr[j] = jnp.broadcast_to(col, (SEQ, QC))

    m_scr[...] = jnp.full(m_scr.shape, -jnp.inf, F32)
    l_scr[...] = jnp.zeros(l_scr.shape, F32)
    acc_scr[...] = jnp.zeros(acc_scr.shape, F32)

    def scores(j, c, kt):
        ks = pl.multiple_of(kt * QC, QC)
        return _dot(k_ref[0, j, pl.ds(ks, QC), :], qt_ref[0, j, c])

    def update(j, c, kt, diagonal, t):
        cs = slice(c * QC, (c + 1) * QC)
        ks = pl.multiple_of(kt * QC, QC)
        t = t - ck_scr[j, pl.ds(ks, QC), :]
        if diagonal:
            key = lax.broadcasted_iota(jnp.int32, (QC, QC), 0)
            qry = lax.broadcasted_iota(jnp.int32, (QC, QC), 1)
            t = jnp.where(key <= qry, t, -jnp.inf)
        c_q = crow_ref[0, j, :, cs] * LOG2E
        m_old = m_scr[j, :, cs]
        m_new = jnp.maximum(m_old, c_q + jnp.max(t, axis=0, keepdims=True))
        alpha = jnp.exp2(m_old - m_new)
        p = jnp.exp2(t + (c_q - m_new))
        l_scr[j, :, cs] = alpha * l_scr[j, :, cs] + jnp.sum(p, axis=0, keepdims=True)
        pv = _dot(vt_ref[0, j, kt], p.astype(BF16))
        acc_scr[j, :, cs] = alpha * acc_scr[j, :, cs] + pv
        m_scr[j, :, cs] = m_new

    def run(blocks):
        ahead = 4
        ts = [scores(*b[:3]) for b in blocks[:ahead]]
        for n, (j, c, kt, diagonal) in enumerate(blocks):
            if n + ahead < len(blocks):
                ts.append(scores(*blocks[n + ahead][:3]))
            update(j, c, kt, diagonal, ts[n])

    def full_tile(kj, carry):
        run([(j, c, kj * nc + s, False)
             for s in range(nc) for j in range(nh) for c in range(nc)])
        return carry

    lax.fori_loop(0, qi, full_tile, 0)
    run([(j, c, qi * nc + s, c == s)
         for s in range(nc) for j in range(nh) for c in range(s, nc)])
    for j in range(nh):
        o = acc_scr[j] / l_scr[j]
        o_ref[0, :, j * C_HEAD_DIM:(j + 1) * C_HEAD_DIM] = o.T.astype(BF16)


def _fox_attention(qt, k, vt, c, *, tq=512, nh=4):
    bsz = k.shape[0]
    nc = tq // QC
    c_row = jnp.transpose(c[:, :, :C_HEADS], (0, 2, 1)).reshape(bsz, C_HEADS, 1, SEQ)
    return pl.pallas_call(
        functools.partial(_fox_attention_kernel, tq=tq, nh=nh),
        grid=(bsz, C_HEADS // nh, SEQ // tq),
        in_specs=[
            pl.BlockSpec((1, nh, nc, C_HEAD_DIM, QC), lambda b, h, i: (b, h, i, 0, 0)),
            pl.BlockSpec((1, nh, SEQ, C_HEAD_DIM), lambda b, h, i: (b, h, 0, 0)),
            pl.BlockSpec((1, nh, SEQ // QC, C_HEAD_DIM, QC), lambda b, h, i: (b, h, 0, 0, 0)),
            pl.BlockSpec((1, SEQ, LANES), lambda b, h, i: (b, 0, 0)),
            pl.BlockSpec((1, nh, 1, tq), lambda b, h, i: (b, h, 0, i)),
        ],
        out_specs=pl.BlockSpec((1, tq, nh * C_HEAD_DIM), lambda b, h, i: (b, i, h)),
        out_shape=jax.ShapeDtypeStruct((bsz, SEQ, C_WIDTH), BF16),
        scratch_shapes=[
            pltpu.VMEM((nh, SEQ, QC), F32),
            pltpu.VMEM((nh, 1, tq), F32),
            pltpu.VMEM((nh, 1, tq), F32),
            pltpu.VMEM((nh, C_HEAD_DIM, tq), F32),
        ],
        compiler_params=_params(3),
        name="fox_attention",
    )(qt, k, vt, c, c_row)


def _proj_residual_kernel(x_ref, a_ref, w_ref, o_ref):
    o_ref[0] = x_ref[0] + _dot(a_ref[0], w_ref[...])


def _proj_residual(x, a, w, *, tm=512):
    bsz = x.shape[0]
    row_spec = pl.BlockSpec((1, tm, D_MODEL), lambda b, i: (b, i, 0))
    return pl.pallas_call(
        _proj_residual_kernel,
        grid=(bsz, SEQ // tm),
        in_specs=[row_spec, pl.BlockSpec((1, tm, a.shape[2]), lambda b, i: (b, i, 0)),
                  _resident(w.shape)],
        out_specs=row_spec,
        out_shape=jax.ShapeDtypeStruct(x.shape, F32),
        compiler_params=_params(2),
        name="proj_residual",
    )(x, a, w.astype(BF16))


def kernel(x, norm_mix_g, norm_ffn_g, final_norm_g, w_in_even, ln_v_g, ln_v_b, w_spatial,
           b_spatial, w_pool, pool_scale, w_out_even, w_in_odd, b_forget, w_out_odd, w_up,
           conv_w, conv_b, w_down):
    depth = norm_mix_g.shape[0]
    assert x.shape[1:] == (SEQ, D_MODEL)
    for i in range(depth):
        if i % 2 == 0:
            e = i // 2
            x = _even_mixer(x, norm_mix_g[i], w_in_even[e], ln_v_g[e], ln_v_b[e], w_spatial[e],
                            b_spatial[e], w_pool[e], pool_scale[e], w_out_even[e])
        else:
            o = i // 2
            qt, k, vt, c = _qkv_proj(x, norm_mix_g[i], w_in_odd[o], b_forget[o])
            att = _fox_attention(qt, k, vt, c)
            x = _proj_residual(x, att, w_out_odd[o])
        x = _conv_ffn(x, norm_ffn_g[i], w_up[i], conv_w[i], conv_b[i], w_down[i],
                      final_norm_g if i == depth - 1 else None)
    if depth == 0:
        raise NotImplementedError("depth 0 is not a configuration of this problem")
    return x
```

```python
import functools

import jax
import jax.numpy as jnp
from jax import lax
from jax.experimental import pallas as pl
from jax.experimental.pallas import tpu as pltpu

D_MODEL = 2048
SEQ = 2048
EPS = 1e-6
CHUNK = 64
A_BLOCK = 128
A_GROUPS = 4
A_WIDTH = D_MODEL // 2
A_GROUP_DIM = A_WIDTH // A_GROUPS
B_WINDOWS = (2, 4, 8, 16)
B_WIDTH = D_MODEL // 2
B_GROUP_DIM = B_WIDTH // len(B_WINDOWS)
C_HEADS = 16
C_HEAD_DIM = 128
C_WIDTH = C_HEADS * C_HEAD_DIM
D_FF = 5632
CONV_WIDTH = 3

LANES = 128
SUBLANES = 8
VMEM_LIMIT_BYTES = 60 * 1024 * 1024

POOL_HALO = 16
CONV_HALO = SUBLANES
FFN_ROWS = 1024
FFN_COLS = 256
QC = 256

BF16 = jnp.bfloat16
F32 = jnp.float32
LOG2E = 1.4426950408889634


def _dot(a, b):
    return jnp.dot(a, b, preferred_element_type=F32)


def _rms_norm(x, g):
    return x * lax.rsqrt(jnp.mean(x * x, axis=-1, keepdims=True) + EPS) * g


def _gelu_exact(x):
    return 0.5 * x * (1.0 + lax.erf(x * (0.5 ** 0.5)))


def _resident(shape):
    zeros = (0,) * len(shape)
    return pl.BlockSpec(shape, lambda *_: zeros, pipeline_mode=pl.Buffered(1))


def _params(n_grid):
    return pltpu.CompilerParams(
        dimension_semantics=("arbitrary",) * n_grid,
        vmem_limit_bytes=VMEM_LIMIT_BYTES,
    )


def _even_mixer_kernel(x_ref, g_ref, win_ref, lng_ref, lnb_ref, ws_ref, bs_ref,
                       wpool_ref, pscale_ref, wout_ref, o_ref, pool_ext, mix_scr, *, tm):
    si = pl.program_id(1)
    x = x_ref[0]
    h = _rms_norm(x, g_ref[...]).astype(BF16)

    u = _gelu_exact(_dot(h, win_ref[:, 0:A_WIDTH]))
    v = _gelu_exact(_dot(h, win_ref[:, A_WIDTH:2 * A_WIDTH]))
    t_idx = lax.broadcasted_iota(jnp.int32, (A_BLOCK, A_BLOCK), 0) // CHUNK
    s_idx = lax.broadcasted_iota(jnp.int32, (A_BLOCK, A_BLOCK), 1) // CHUNK
    causal = t_idx >= s_idx
    for g in range(A_GROUPS):
        cols = slice(g * A_GROUP_DIM, (g + 1) * A_GROUP_DIM)
        vg = v[:, cols]
        mu = jnp.mean(vg, axis=-1, keepdims=True)
        var = jnp.mean(jnp.square(vg - mu), axis=-1, keepdims=True)
        vn = ((vg - mu) * lax.rsqrt(var + EPS) * lng_ref[:, cols] + lnb_ref[:, cols]).astype(BF16)
        w_g = jnp.where(causal, ws_ref[g], 0.0).astype(BF16)
        for n in range(tm // A_BLOCK):
            rows = slice(n * A_BLOCK, (n + 1) * A_BLOCK)
            gate = _dot(w_g, vn[rows]) + bs_ref[g]
            mix_scr[rows, cols] = (u[rows, cols] * gate).astype(BF16)

    @pl.when(si == 0)
    def _():
        pool_ext[0:POOL_HALO, :] = jnp.zeros((POOL_HALO, B_WIDTH), F32)

    @pl.when(si > 0)
    def _():
        pool_ext[0:POOL_HALO, :] = pool_ext[tm:tm + POOL_HALO, :]

    pool_ext[POOL_HALO:POOL_HALO + tm, :] = _dot(h, win_ref[:, 2 * A_WIDTH:2 * A_WIDTH + B_WIDTH])
    pos = si * tm + lax.broadcasted_iota(jnp.int32, (tm, 1), 0)
    for j, w in enumerate(B_WINDOWS):
        cols = slice(j * B_GROUP_DIM, (j + 1) * B_GROUP_DIM)
        xg = pool_ext[POOL_HALO:POOL_HALO + tm, cols]
        acc = xg
        for k in range(1, w):
            acc = acc + pool_ext[POOL_HALO - k:POOL_HALO - k + tm, cols]
        count = jnp.minimum(pos + 1, w).astype(F32)
        d = (acc / count - xg).astype(BF16)
        yb = _dot(d, wpool_ref[j]) * pscale_ref[:, cols]
        mix_scr[:, A_WIDTH + j * B_GROUP_DIM:A_WIDTH + (j + 1) * B_GROUP_DIM] = yb.astype(BF16)

    o_ref[0] = x + _dot(mix_scr[...], wout_ref[...])


def _even_mixer(x, g, w_in, ln_g, ln_b, w_s, b_s, w_pool, p_scale, w_out, *, tm=256):
    bsz = x.shape[0]
    even_in = 2 * A_WIDTH + B_WIDTH
    row_spec = pl.BlockSpec((1, tm, D_MODEL), lambda b, i: (b, i, 0))
    return pl.pallas_call(
        functools.partial(_even_mixer_kernel, tm=tm),
        grid=(bsz, SEQ // tm),
        in_specs=[
            row_spec,
            _resident((1, D_MODEL)),
            _resident((D_MODEL, even_in)),
            _resident((1, A_WIDTH)),
            _resident((1, A_WIDTH)),
            _resident((A_GROUPS, A_BLOCK, A_BLOCK)),
            _resident((A_GROUPS, A_BLOCK, 1)),
            _resident((len(B_WINDOWS), B_GROUP_DIM, B_GROUP_DIM)),
            _resident((1, B_WIDTH)),
            _resident((A_WIDTH + B_WIDTH, D_MODEL)),
        ],
        out_specs=row_spec,
        out_shape=jax.ShapeDtypeStruct(x.shape, F32),
        scratch_shapes=[
            pltpu.VMEM((tm + POOL_HALO, B_WIDTH), F32),
            pltpu.VMEM((tm, A_WIDTH + B_WIDTH), BF16),
        ],
        compiler_params=_params(2),
        name="even_mixer",
    )(x, g.reshape(1, D_MODEL), w_in.astype(BF16), ln_g.reshape(1, A_WIDTH),
      ln_b.reshape(1, A_WIDTH), w_s, b_s.reshape(A_GROUPS, A_BLOCK, 1),
      w_pool.astype(BF16), p_scale.reshape(1, B_WIDTH), w_out.astype(BF16))


def _conv_ffn_kernel(x_ref, g_ref, wg_ref, wv_ref, cwg_ref, cwv_ref, cbg_ref, cbv_ref,
                     wd_ref, *rest, tm, nf, final_norm):
    if final_norm:
        gfin_ref, o_ref, h_scr, ext_a, ext_b, act_scr, carry = rest
    else:
        o_ref, h_scr, ext_a, ext_b, act_scr, carry = rest
    ext = (ext_a, ext_b)
    b = pl.program_id(0)
    si = pl.program_id(1)
    f = pl.program_id(2)

    row_blocks = [slice(r, r + FFN_ROWS) for r in range(0, tm, FFN_ROWS)]

    tf = wg_ref.shape[1]
    n_pieces = 2 * tf // FFN_COLS
    act_cols = tf // n_pieces

    def up_piece(ext_w, rows, q):
        part, c0 = divmod(q * FFN_COLS, tf)
        cols = slice(c0, c0 + FFN_COLS)
        w_ref = (wg_ref, wv_ref)[part]
        up = _dot(h_scr[rows, :], w_ref[:, cols])
        if rows.start == 0:
            ext_w[part, 0:CONV_HALO, cols] = jnp.where(si > 0, carry[f, part, :, cols], 0.0)
        ext_w[part, CONV_HALO + rows.start:CONV_HALO + rows.stop, cols] = up
        if rows.stop == tm:
            carry[f, part, :, cols] = up[FFN_ROWS - CONV_HALO:FFN_ROWS, :]

    def act_piece(ext_r, rows, q):
        cols = slice(q * act_cols, (q + 1) * act_cols)

        def causal_conv(part, cw_ref, cb_ref):
            lo = rows.start
            e = ext_r[part, lo:lo + CONV_HALO + FFN_ROWS, cols]
            out = cb_ref[:, cols]
            for j in range(CONV_WIDTH):
                lag = CONV_WIDTH - 1 - j
                lagged = pltpu.roll(e, lag, axis=0) if lag else e
                out = out + lagged[CONV_HALO:, :] * cw_ref[j:j + 1, cols]
            return out

        cg = causal_conv(0, cwg_ref, cbg_ref)
        cv = causal_conv(1, cwv_ref, cbv_ref)
        return (cg * (1.0 / (1.0 + jnp.exp(-cg))) * cv).astype(BF16)

    def step(p, *, up, down, first=False, last=False):
        for rows in row_blocks:
            if first:
                x = x_ref[0, rows, :]
                h_scr[rows, :] = _rms_norm(x, g_ref[...]).astype(BF16)
                o_ref[0, rows, :] = x
            for q in range(n_pieces):
                if down:
                    act_scr[rows, q * act_cols:(q + 1) * act_cols] = act_piece(ext[1 - p], rows, q)
                if up:
                    up_piece(ext[p], rows, q)
            if down:
                out = o_ref[0, rows, :] + _dot(act_scr[rows, :], wd_ref[...])
                if last and final_norm:
                    out = _rms_norm(out, gfin_ref[...])
                o_ref[0, rows, :] = out

    @pl.when(f == 0)
    def _():
        @pl.when((b == 0) & (si == 0))
        def _():
            carry[...] = jnp.zeros(carry.shape, F32)

        step(0, up=True, down=False, first=True)

    for p in range(2):
        @pl.when((f >= 1) & (f < nf) & (f % 2 == p))
        def _():
            step(p, up=True, down=True)

    @pl.when(f == nf)
    def _():
        step(nf % 2, up=False, down=True, last=True)


def _conv_ffn(x, g, w_up, conv_w, conv_b, w_down, g_final=None, *, tm=1024, tf=256):
    bsz = x.shape[0]
    nf = D_FF // tf
    final_norm = g_final is not None
    row_spec = pl.BlockSpec((1, tm, D_MODEL), lambda b, i, f: (b, i, 0))
    w_up = w_up.astype(BF16)
    conv_b = conv_b.reshape(1, 2 * D_FF)

    def chunk(f, lag):
        return jnp.clip(f - lag, 0, nf - 1)

    in_specs = [
        row_spec,
        pl.BlockSpec((1, D_MODEL), lambda b, i, f: (0, 0)),
        pl.BlockSpec((D_MODEL, tf), lambda b, i, f: (0, chunk(f, 0))),
        pl.BlockSpec((D_MODEL, tf), lambda b, i, f: (0, nf + chunk(f, 0))),
        pl.BlockSpec((CONV_WIDTH, tf), lambda b, i, f: (0, chunk(f, 1))),
        pl.BlockSpec((CONV_WIDTH, tf), lambda b, i, f: (0, nf + chunk(f, 1))),
        pl.BlockSpec((1, tf), lambda b, i, f: (0, chunk(f, 1))),
        pl.BlockSpec((1, tf), lambda b, i, f: (0, nf + chunk(f, 1))),
        pl.BlockSpec((tf, D_MODEL), lambda b, i, f: (chunk(f, 1), 0)),
    ]
    args = [x, g.reshape(1, D_MODEL), w_up, w_up, conv_w, conv_w, conv_b, conv_b,
            w_down.astype(BF16)]
    if final_norm:
        in_specs.append(pl.BlockSpec((1, D_MODEL), lambda b, i, f: (0, 0)))
        args.append(g_final.reshape(1, D_MODEL))
    return pl.pallas_call(
        functools.partial(_conv_ffn_kernel, tm=tm, nf=nf, final_norm=final_norm),
        grid=(bsz, SEQ // tm, nf + 1),
        in_specs=in_specs,
        out_specs=row_spec,
        out_shape=jax.ShapeDtypeStruct(x.shape, F32),
        scratch_shapes=[
            pltpu.VMEM((tm, D_MODEL), BF16),
            pltpu.VMEM((2, tm + CONV_HALO, tf), F32),
            pltpu.VMEM((2, tm + CONV_HALO, tf), F32),
            pltpu.VMEM((tm, tf), BF16),
            pltpu.VMEM((nf, 2, CONV_HALO, tf), F32),
        ],
        compiler_params=_params(3),
        name="conv_ffn_final" if final_norm else "conv_ffn",
    )(*args)


def _log_sigmoid(x):
    return jnp.minimum(x, 0.0) - jnp.log1p(jnp.exp(-jnp.abs(x)))


def _qkv_kernel(x_ref, g_ref, wqt_ref, wk_ref, wvt_ref, wf_ref, bf_ref,
                qt_ref, k_ref, vt_ref, c_ref, carry, *, tm):
    si = pl.program_id(1)
    hf = _rms_norm(x_ref[0], g_ref[...])
    h = hf.astype(BF16)
    ht = hf.T.astype(BF16)
    q_scale = C_HEAD_DIM ** -0.5 * LOG2E
    kk = _dot(h, wk_ref[...])
    qt = _dot(wqt_ref[...], ht) * q_scale
    vt = _dot(wvt_ref[...], ht)
    for hd in range(C_HEADS):
        dims = slice(hd * C_HEAD_DIM, (hd + 1) * C_HEAD_DIM)
        k_ref[0, hd] = kk[:, dims].astype(BF16)
        qt_ref[0, hd, 0] = qt[dims, :].astype(BF16)
        vt_ref[0, hd, 0] = vt[dims, :].astype(BF16)

    log_f = _log_sigmoid(_dot(h, wf_ref[...]) + bf_ref[...])
    row = lax.broadcasted_iota(jnp.int32, (tm, LANES), 0)
    k = 1
    while k < tm:
        log_f = log_f + jnp.where(row >= k, pltpu.roll(log_f, k, axis=0), 0.0)
        k *= 2

    @pl.when(si == 0)
    def _():
        carry[...] = jnp.zeros(carry.shape, F32)

    c = log_f + carry[0:1, :]
    c_ref[0] = c
    carry[...] = jnp.broadcast_to(c[tm - 1:tm, :], carry.shape)


def _qkv_proj(x, g, w_in, b_forget):
    bsz = x.shape[0]
    tm = QC
    wq_t = w_in[:, :C_WIDTH].T.astype(BF16)
    wk = w_in[:, C_WIDTH:2 * C_WIDTH].astype(BF16)
    wv_t = w_in[:, 2 * C_WIDTH:3 * C_WIDTH].T.astype(BF16)
    w_f = jnp.pad(w_in[:, 3 * C_WIDTH:], ((0, 0), (0, LANES - C_HEADS))).astype(BF16)
    b_f = jnp.pad(b_forget, (0, LANES - C_HEADS)).reshape(1, LANES)
    t_spec = pl.BlockSpec((1, C_HEADS, 1, C_HEAD_DIM, QC), lambda b, i: (b, 0, i, 0, 0))
    t_shape = jax.ShapeDtypeStruct((bsz, C_HEADS, SEQ // QC, C_HEAD_DIM, QC), BF16)
    return pl.pallas_call(
        functools.partial(_qkv_kernel, tm=tm),
        grid=(bsz, SEQ // tm),
        in_specs=[
            pl.BlockSpec((1, tm, D_MODEL), lambda b, i: (b, i, 0)),
            _resident((1, D_MODEL)),
            _resident((C_WIDTH, D_MODEL)),
            _resident((D_MODEL, C_WIDTH)),
            _resident((C_WIDTH, D_MODEL)),
            _resident((D_MODEL, LANES)),
            _resident((1, LANES)),
        ],
        out_specs=[t_spec,
                   pl.BlockSpec((1, C_HEADS, tm, C_HEAD_DIM), lambda b, i: (b, 0, i, 0)),
                   t_spec,
                   pl.BlockSpec((1, tm, LANES), lambda b, i: (b, i, 0))],
        out_shape=[t_shape,
                   jax.ShapeDtypeStruct((bsz, C_HEADS, SEQ, C_HEAD_DIM), BF16),
                   t_shape,
                   jax.ShapeDtypeStruct((bsz, SEQ, LANES), F32)],
        scratch_shapes=[pltpu.VMEM((SUBLANES, LANES), F32)],
        compiler_params=_params(2),
        name="qkv_proj",
    )(x, g.reshape(1, D_MODEL), wq_t, wk, wv_t, w_f, b_f)


def _fox_attention_kernel(qt_ref, k_ref, vt_ref, c_ref, crow_ref, o_ref,
                          ck_scr, m_scr, l_scr, acc_scr, *, tq, nh):
    hp = pl.program_id(1)
    qi = pl.program_id(2)
    nc = tq // QC

    @pl.when(qi == 0)
    def _():
        lane = lax.broadcasted_iota(jnp.int32, (SEQ, LANES), 1)
        c_all = c_ref[0] * LOG2E
        for j in range(nh):
            col = jnp.sum(jnp.where(lane == hp * nh + j, c_all, 0.0), axis=1, keepdims=True)
            ck_scr[j] = jnp.broadcast_to(col, (SEQ, QC))

    m_scr[...] = jnp.full(m_scr.shape, -jnp.inf, F32)
    l_scr[...] = jnp.zeros(l_scr.shape, F32)
    acc_scr[...] = jnp.zeros(acc_scr.shape, F32)

    def scores(j, c, kt):
        ks = pl.multiple_of(kt * QC, QC)
        return _dot(k_ref[0, j, pl.ds(ks, QC), :], qt_ref[0, j, c])

    def update(j, c, kt, diagonal, t):
        cs = slice(c * QC, (c + 1) * QC)
        ks = pl.multiple_of(kt * QC, QC)
        t = t - ck_scr[j, pl.ds(ks, QC), :]
        if diagonal:
            key = lax.broadcasted_iota(jnp.int32, (QC, QC), 0)
            qry = lax.broadcasted_iota(jnp.int32, (QC, QC), 1)
            t = jnp.where(key <= qry, t, -jnp.inf)
        c_q = crow_ref[0, j, :, cs] * LOG2E
        m_old = m_scr[j, :, cs]
        m_new = jnp.maximum(m_old, c_q + jnp.max(t, axis=0, keepdims=True))
        alpha = jnp.exp2(m_old - m_new)
        p = jnp.exp2(t + (c_q - m_new))
        l_scr[j, :, cs] = alpha * l_scr[j, :, cs] + jnp.sum(p, axis=0, keepdims=True)
        pv = _dot(vt_ref[0, j, kt], p.astype(BF16))
        acc_scr[j, :, cs] = alpha * acc_scr[j, :, cs] + pv
        m_scr[j, :, cs] = m_new

    def run(blocks):
        ahead = 4
        ts = [scores(*b[:3]) for b in blocks[:ahead]]
        for n, (j, c, kt, diagonal) in enumerate(blocks):
            if n + ahead < len(blocks):
                ts.append(scores(*blocks[n + ahead][:3]))
            update(j, c, kt, diagonal, ts[n])

    def full_tile(kj, carry):
        run([(j, c, kj * nc + s, False)
             for s in range(nc) for j in range(nh) for c in range(nc)])
        return carry

    lax.fori_loop(0, qi, full_tile, 0)
    run([(j, c, qi * nc + s, c == s)
         for s in range(nc) for j in range(nh) for c in range(s, nc)])
    for j in range(nh):
        o = acc_scr[j] / l_scr[j]
        o_ref[0, :, j * C_HEAD_DIM:(j + 1) * C_HEAD_DIM] = o.T.astype(BF16)


def _fox_attention(qt, k, vt, c, *, tq=512, nh=4):
    bsz = k.shape[0]
    nc = tq // QC
    c_row = jnp.transpose(c[:, :, :C_HEADS], (0, 2, 1)).reshape(bsz, C_HEADS, 1, SEQ)
    return pl.pallas_call(
        functools.partial(_fox_attention_kernel, tq=tq, nh=nh),
        grid=(bsz, C_HEADS // nh, SEQ // tq),
        in_specs=[
            pl.BlockSpec((1, nh, nc, C_HEAD_DIM, QC), lambda b, h, i: (b, h, i, 0, 0)),
            pl.BlockSpec((1, nh, SEQ, C_HEAD_DIM), lambda b, h, i: (b, h, 0, 0)),
            pl.BlockSpec((1, nh, SEQ // QC, C_HEAD_DIM, QC), lambda b, h, i: (b, h, 0, 0, 0)),
            pl.BlockSpec((1, SEQ, LANES), lambda b, h, i: (b, 0, 0)),
            pl.BlockSpec((1, nh, 1, tq), lambda b, h, i: (b, h, 0, i)),
        ],
        out_specs=pl.BlockSpec((1, tq, nh * C_HEAD_DIM), lambda b, h, i: (b, i, h)),
        out_shape=jax.ShapeDtypeStruct((bsz, SEQ, C_WIDTH), BF16),
        scratch_shapes=[
            pltpu.VMEM((nh, SEQ, QC), F32),
            pltpu.VMEM((nh, 1, tq), F32),
            pltpu.VMEM((nh, 1, tq), F32),
            pltpu.VMEM((nh, C_HEAD_DIM, tq), F32),
        ],
        compiler_params=_params(3),
        name="fox_attention",
    )(qt, k, vt, c, c_row)


def _proj_residual_kernel(x_ref, a_ref, w_ref, o_ref):
    o_ref[0] = x_ref[0] + _dot(a_ref[0], w_ref[...])


def _proj_residual(x, a, w, *, tm=512):
    bsz = x.shape[0]
    row_spec = pl.BlockSpec((1, tm, D_MODEL), lambda b, i: (b, i, 0))
    return pl.pallas_call(
        _proj_residual_kernel,
        grid=(bsz, SEQ // tm),
        in_specs=[row_spec, pl.BlockSpec((1, tm, a.shape[2]), lambda b, i: (b, i, 0)),
                  _resident(w.shape)],
        out_specs=row_spec,
        out_shape=jax.ShapeDtypeStruct(x.shape, F32),
        compiler_params=_params(2),
        name="proj_residual",
    )(x, a, w.astype(BF16))


def kernel(x, norm_mix_g, norm_ffn_g, final_norm_g, w_in_even, ln_v_g, ln_v_b, w_spatial,
           b_spatial, w_pool, pool_scale, w_out_even, w_in_odd, b_forget, w_out_odd, w_up,
           conv_w, conv_b, w_down):
    depth = norm_mix_g.shape[0]
    assert x.shape[1:] == (SEQ, D_MODEL)
    for i in range(depth):
        if i % 2 == 0:
            e = i // 2
            x = _even_mixer(x, norm_mix_g[i], w_in_even[e], ln_v_g[e], ln_v_b[e], w_spatial[e],
                            b_spatial[e], w_pool[e], pool_scale[e], w_out_even[e])
        else:
            o = i // 2
            qt, k, vt, c = _qkv_proj(x, norm_mix_g[i], w_in_odd[o], b_forget[o])
            att = _fox_attention(qt, k, vt, c)
            x = _proj_residual(x, att, w_out_odd[o])
        x = _conv_ffn(x, norm_ffn_g[i], w_up[i], conv_w[i], conv_b[i], w_down[i],
                      final_norm_g if i == depth - 1 else None)
    if depth == 0:
        raise NotImplementedError("depth 0 is not a configuration of this problem")
    return x
```

```python
import functools

import jax
import jax.numpy as jnp
from jax import lax
from jax.experimental import pallas as pl
from jax.experimental.pallas import tpu as pltpu

D_MODEL = 2048
SEQ = 2048
EPS = 1e-6
CHUNK = 64
A_BLOCK = 128
A_GROUPS = 4
A_WIDTH = D_MODEL // 2
A_GROUP_DIM = A_WIDTH // A_GROUPS
B_WINDOWS = (2, 4, 8, 16)
B_WIDTH = D_MODEL // 2
B_GROUP_DIM = B_WIDTH // len(B_WINDOWS)
C_HEADS = 16
C_HEAD_DIM = 128
C_WIDTH = C_HEADS * C_HEAD_DIM
D_FF = 5632
CONV_WIDTH = 3

LANES = 128
SUBLANES = 8
VMEM_LIMIT_BYTES = 60 * 1024 * 1024

POOL_HALO = 16
CONV_HALO = SUBLANES
FFN_ROWS = 256
FFN_COLS = 256
QC = 256

BF16 = jnp.bfloat16
F32 = jnp.float32
LOG2E = 1.4426950408889634


def _dot(a, b):
    return jnp.dot(a, b, preferred_element_type=F32)


def _rms_norm(x, g):
    return x * lax.rsqrt(jnp.mean(x * x, axis=-1, keepdims=True) + EPS) * g


def _gelu_exact(x):
    return 0.5 * x * (1.0 + lax.erf(x * (0.5 ** 0.5)))


def _resident(shape):
    zeros = (0,) * len(shape)
    return pl.BlockSpec(shape, lambda *_: zeros, pipeline_mode=pl.Buffered(1))


def _params(n_grid):
    return pltpu.CompilerParams(
        dimension_semantics=("arbitrary",) * n_grid,
        vmem_limit_bytes=VMEM_LIMIT_BYTES,
    )


def _even_mixer_kernel(x_ref, g_ref, win_ref, lng_ref, lnb_ref, ws_ref, bs_ref,
                       wpool_ref, pscale_ref, wout_ref, o_ref, pool_ext, mix_scr, *, tm):
    si = pl.program_id(1)
    x = x_ref[0]
    h = _rms_norm(x, g_ref[...]).astype(BF16)

    u = _gelu_exact(_dot(h, win_ref[:, 0:A_WIDTH]))
    v = _gelu_exact(_dot(h, win_ref[:, A_WIDTH:2 * A_WIDTH]))
    t_idx = lax.broadcasted_iota(jnp.int32, (A_BLOCK, A_BLOCK), 0) // CHUNK
    s_idx = lax.broadcasted_iota(jnp.int32, (A_BLOCK, A_BLOCK), 1) // CHUNK
    causal = t_idx >= s_idx
    for g in range(A_GROUPS):
        cols = slice(g * A_GROUP_DIM, (g + 1) * A_GROUP_DIM)
        vg = v[:, cols]
        mu = jnp.mean(vg, axis=-1, keepdims=True)
        var = jnp.mean(jnp.square(vg - mu), axis=-1, keepdims=True)
        vn = ((vg - mu) * lax.rsqrt(var + EPS) * lng_ref[:, cols] + lnb_ref[:, cols]).astype(BF16)
        w_g = jnp.where(causal, ws_ref[g], 0.0).astype(BF16)
        for n in range(tm // A_BLOCK):
            rows = slice(n * A_BLOCK, (n + 1) * A_BLOCK)
            gate = _dot(w_g, vn[rows]) + bs_ref[g]
            mix_scr[rows, cols] = (u[rows, cols] * gate).astype(BF16)

    @pl.when(si == 0)
    def _():
        pool_ext[0:POOL_HALO, :] = jnp.zeros((POOL_HALO, B_WIDTH), F32)

    @pl.when(si > 0)
    def _():
        pool_ext[0:POOL_HALO, :] = pool_ext[tm:tm + POOL_HALO, :]

    pool_ext[POOL_HALO:POOL_HALO + tm, :] = _dot(h, win_ref[:, 2 * A_WIDTH:2 * A_WIDTH + B_WIDTH])
    pos = si * tm + lax.broadcasted_iota(jnp.int32, (tm, 1), 0)
    for j, w in enumerate(B_WINDOWS):
        cols = slice(j * B_GROUP_DIM, (j + 1) * B_GROUP_DIM)
        xg = pool_ext[POOL_HALO:POOL_HALO + tm, cols]
        acc = xg
        for k in range(1, w):
            acc = acc + pool_ext[POOL_HALO - k:POOL_HALO - k + tm, cols]
        count = jnp.minimum(pos + 1, w).astype(F32)
        d = (acc / count - xg).astype(BF16)
        yb = _dot(d, wpool_ref[j]) * pscale_ref[:, cols]
        mix_scr[:, A_WIDTH + j * B_GROUP_DIM:A_WIDTH + (j + 1) * B_GROUP_DIM] = yb.astype(BF16)

    o_ref[0] = x + _dot(mix_scr[...], wout_ref[...])


def _even_mixer(x, g, w_in, ln_g, ln_b, w_s, b_s, w_pool, p_scale, w_out, *, tm=256):
    bsz = x.shape[0]
    even_in = 2 * A_WIDTH + B_WIDTH
    row_spec = pl.BlockSpec((1, tm, D_MODEL), lambda b, i: (b, i, 0))
    return pl.pallas_call(
        functools.partial(_even_mixer_kernel, tm=tm),
        grid=(bsz, SEQ // tm),
        in_specs=[
            row_spec,
            _resident((1, D_MODEL)),
            _resident((D_MODEL, even_in)),
            _resident((1, A_WIDTH)),
            _resident((1, A_WIDTH)),
            _resident((A_GROUPS, A_BLOCK, A_BLOCK)),
            _resident((A_GROUPS, A_BLOCK, 1)),
            _resident((len(B_WINDOWS), B_GROUP_DIM, B_GROUP_DIM)),
            _resident((1, B_WIDTH)),
            _resident((A_WIDTH + B_WIDTH, D_MODEL)),
        ],
        out_specs=row_spec,
        out_shape=jax.ShapeDtypeStruct(x.shape, F32),
        scratch_shapes=[
            pltpu.VMEM((tm + POOL_HALO, B_WIDTH), F32),
            pltpu.VMEM((tm, A_WIDTH + B_WIDTH), BF16),
        ],
        compiler_params=_params(2),
        name="even_mixer",
    )(x, g.reshape(1, D_MODEL), w_in.astype(BF16), ln_g.reshape(1, A_WIDTH),
      ln_b.reshape(1, A_WIDTH), w_s, b_s.reshape(A_GROUPS, A_BLOCK, 1),
      w_pool.astype(BF16), p_scale.reshape(1, B_WIDTH), w_out.astype(BF16))


def _conv_ffn_kernel(x_ref, g_ref, wg_ref, wv_ref, cwg_ref, cwv_ref, cbg_ref, cbv_ref,
                     wd_ref, *rest, tm, nf, final_norm):
    if final_norm:
        gfin_ref, o_ref, h_scr, ext_a, act_scr, carry = rest
    else:
        o_ref, h_scr, ext_a, act_scr, carry = rest
    b = pl.program_id(0)
    si = pl.program_id(1)
    f = pl.program_id(2)

    row_blocks = [slice(r, r + FFN_ROWS) for r in range(0, tm, FFN_ROWS)]

    tf = wg_ref.shape[1]
    n_pieces = 2 * tf // FFN_COLS
    act_cols = tf // n_pieces

    def up_piece(ext_w, rows, q):
        part, c0 = divmod(q * FFN_COLS, tf)
        cols = slice(c0, c0 + FFN_COLS)
        w_ref = (wg_ref, wv_ref)[part]
        up = _dot(h_scr[rows, :], w_ref[:, cols])
        if rows.start == 0:
            ext_w[part, 0:CONV_HALO, cols] = jnp.where(si > 0, carry[f, part, :, cols], 0.0)
        ext_w[part, CONV_HALO + rows.start:CONV_HALO + rows.stop, cols] = up
        if rows.stop == tm:
            carry[f, part, :, cols] = up[FFN_ROWS - CONV_HALO:FFN_ROWS, :]

    def act_piece(ext_r, rows, q):
        cols = slice(q * act_cols, (q + 1) * act_cols)

        def causal_conv(part, cw_ref, cb_ref):
            lo = rows.start
            e = ext_r[part, lo:lo + CONV_HALO + FFN_ROWS, cols]
            out = cb_ref[:, cols]
            for j in range(CONV_WIDTH):
                lag = CONV_WIDTH - 1 - j
                lagged = pltpu.roll(e, lag, axis=0) if lag else e
                out = out + lagged[CONV_HALO:, :] * cw_ref[j:j + 1, cols]
            return out

        cg = causal_conv(0, cwg_ref, cbg_ref)
        cv = causal_conv(1, cwv_ref, cbv_ref)
        return (cg * (1.0 / (1.0 + jnp.exp(-cg))) * cv).astype(BF16)

    @pl.when(f == 0)
    def _():
        @pl.when((b == 0) & (si == 0))
        def _():
            carry[...] = jnp.zeros(carry.shape, F32)

        x = x_ref[0]
        h_scr[...] = _rms_norm(x, g_ref[...]).astype(BF16)
        o_ref[0] = x

    for rows in row_blocks:
        for q in range(n_pieces):
            up_piece(ext_a, rows, q)
    for rows in row_blocks:
        for q in range(n_pieces):
            act_scr[rows, q * act_cols:(q + 1) * act_cols] = act_piece(ext_a, rows, q)
        o_ref[0, rows, :] += _dot(act_scr[rows, :], wd_ref[...])

    if final_norm:
        @pl.when(f == nf - 1)
        def _():
            o_ref[0] = _rms_norm(o_ref[0], gfin_ref[...])


def _conv_ffn(x, g, w_up, conv_w, conv_b, w_down, g_final=None, *, tm=512, tf=512):
    bsz = x.shape[0]
    nf = D_FF // tf
    final_norm = g_final is not None
    row_spec = pl.BlockSpec((1, tm, D_MODEL), lambda b, i, f: (b, i, 0))
    w_up = w_up.astype(BF16)
    conv_b = conv_b.reshape(1, 2 * D_FF)

    in_specs = [
        row_spec,
        pl.BlockSpec((1, D_MODEL), lambda b, i, f: (0, 0)),
        pl.BlockSpec((D_MODEL, tf), lambda b, i, f: (0, f)),
        pl.BlockSpec((D_MODEL, tf), lambda b, i, f: (0, nf + f)),
        pl.BlockSpec((CONV_WIDTH, tf), lambda b, i, f: (0, f)),
        pl.BlockSpec((CONV_WIDTH, tf), lambda b, i, f: (0, nf + f)),
        pl.BlockSpec((1, tf), lambda b, i, f: (0, f)),
        pl.BlockSpec((1, tf), lambda b, i, f: (0, nf + f)),
        pl.BlockSpec((tf, D_MODEL), lambda b, i, f: (f, 0)),
    ]
    args = [x, g.reshape(1, D_MODEL), w_up, w_up, conv_w, conv_w, conv_b, conv_b,
            w_down.astype(BF16)]
    if final_norm:
        in_specs.append(pl.BlockSpec((1, D_MODEL), lambda b, i, f: (0, 0)))
        args.append(g_final.reshape(1, D_MODEL))
    return pl.pallas_call(
        functools.partial(_conv_ffn_kernel, tm=tm, nf=nf, final_norm=final_norm),
        grid=(bsz, SEQ // tm, nf),
        in_specs=in_specs,
        out_specs=row_spec,
        out_shape=jax.ShapeDtypeStruct(x.shape, F32),
        scratch_shapes=[
            pltpu.VMEM((tm, D_MODEL), BF16),
            pltpu.VMEM((2, tm + CONV_HALO, tf), F32),
            pltpu.VMEM((tm, tf), BF16),
            pltpu.VMEM((nf, 2, CONV_HALO, tf), F32),
        ],
        compiler_params=_params(3),
        name="conv_ffn_final" if final_norm else "conv_ffn",
    )(*args)


def _log_sigmoid(x):
    return jnp.minimum(x, 0.0) - jnp.log1p(jnp.exp(-jnp.abs(x)))


def _qkv_kernel(x_ref, g_ref, wqt_ref, wk_ref, wvt_ref, wf_ref, bf_ref,
                qt_ref, k_ref, vt_ref, c_ref, carry, *, tm):
    si = pl.program_id(1)
    hf = _rms_norm(x_ref[0], g_ref[...])
    h = hf.astype(BF16)
    ht = hf.T.astype(BF16)
    q_scale = C_HEAD_DIM ** -0.5 * LOG2E
    kk = _dot(h, wk_ref[...])
    qt = _dot(wqt_ref[...], ht) * q_scale
    vt = _dot(wvt_ref[...], ht)
    for hd in range(C_HEADS):
        dims = slice(hd * C_HEAD_DIM, (hd + 1) * C_HEAD_DIM)
        k_ref[0, hd] = kk[:, dims].astype(BF16)
        qt_ref[0, hd, 0] = qt[dims, :].astype(BF16)
        vt_ref[0, hd, 0] = vt[dims, :].astype(BF16)

    log_f = _log_sigmoid(_dot(h, wf_ref[...]) + bf_ref[...])
    row = lax.broadcasted_iota(jnp.int32, (tm, LANES), 0)
    k = 1
    while k < tm:
        log_f = log_f + jnp.where(row >= k, pltpu.roll(log_f, k, axis=0), 0.0)
        k *= 2

    @pl.when(si == 0)
    def _():
        carry[...] = jnp.zeros(carry.shape, F32)

    c = log_f + carry[0:1, :]
    c_ref[0] = c
    carry[...] = jnp.broadcast_to(c[tm - 1:tm, :], carry.shape)


def _qkv_proj(x, g, w_in, b_forget):
    bsz = x.shape[0]
    tm = QC
    wq_t = w_in[:, :C_WIDTH].T.astype(BF16)
    wk = w_in[:, C_WIDTH:2 * C_WIDTH].astype(BF16)
    wv_t = w_in[:, 2 * C_WIDTH:3 * C_WIDTH].T.astype(BF16)
    w_f = jnp.pad(w_in[:, 3 * C_WIDTH:], ((0, 0), (0, LANES - C_HEADS))).astype(BF16)
    b_f = jnp.pad(b_forget, (0, LANES - C_HEADS)).reshape(1, LANES)
    t_spec = pl.BlockSpec((1, C_HEADS, 1, C_HEAD_DIM, QC), lambda b, i: (b, 0, i, 0, 0))
    t_shape = jax.ShapeDtypeStruct((bsz, C_HEADS, SEQ // QC, C_HEAD_DIM, QC), BF16)
    return pl.pallas_call(
        functools.partial(_qkv_kernel, tm=tm),
        grid=(bsz, SEQ // tm),
        in_specs=[
            pl.BlockSpec((1, tm, D_MODEL), lambda b, i: (b, i, 0)),
            _resident((1, D_MODEL)),
            _resident((C_WIDTH, D_MODEL)),
            _resident((D_MODEL, C_WIDTH)),
            _resident((C_WIDTH, D_MODEL)),
            _resident((D_MODEL, LANES)),
            _resident((1, LANES)),
        ],
        out_specs=[t_spec,
                   pl.BlockSpec((1, C_HEADS, tm, C_HEAD_DIM), lambda b, i: (b, 0, i, 0)),
                   t_spec,
                   pl.BlockSpec((1, tm, LANES), lambda b, i: (b, i, 0))],
        out_shape=[t_shape,
                   jax.ShapeDtypeStruct((bsz, C_HEADS, SEQ, C_HEAD_DIM), BF16),
                   t_shape,
                   jax.ShapeDtypeStruct((bsz, SEQ, LANES), F32)],
        scratch_shapes=[pltpu.VMEM((SUBLANES, LANES), F32)],
        compiler_params=_params(2),
        name="qkv_proj",
    )(x, g.reshape(1, D_MODEL), wq_t, wk, wv_t, w_f, b_f)


def _fox_attention_kernel(qt_ref, k_ref, vt_ref, c_ref, crow_ref, o_ref,
                          ck_scr, m_scr, l_scr, acc_scr, *, tq, nh):
    hp = pl.program_id(1)
    qi = pl.program_id(2)
    nc = tq // QC

    @pl.when(qi == 0)
    def _():
        lane = lax.broadcasted_iota(jnp.int32, (SEQ, LANES), 1)
        c_all = c_ref[0] * LOG2E
        for j in range(nh):
            col = jnp.sum(jnp.where(lane == hp * nh + j, c_all, 0.0), axis=1, keepdims=True)
            ck_scr[j] = jnp.broadcast_to(col, (SEQ, QC))

    m_scr[...] = jnp.full(m_scr.shape, -jnp.inf, F32)
    l_scr[...] = jnp.zeros(l_scr.shape, F32)
    acc_scr[...] = jnp.zeros(acc_scr.shape, F32)

    def scores(j, c, kt):
        ks = pl.multiple_of(kt * QC, QC)
        return _dot(k_ref[0, j, pl.ds(ks, QC), :], qt_ref[0, j, c])

    def update(j, c, kt, diagonal, t):
        cs = slice(c * QC, (c + 1) * QC)
        ks = pl.multiple_of(kt * QC, QC)
        t = t - ck_scr[j, pl.ds(ks, QC), :]
        if diagonal:
            key = lax.broadcasted_iota(jnp.int32, (QC, QC), 0)
            qry = lax.broadcasted_iota(jnp.int32, (QC, QC), 1)
            t = jnp.where(key <= qry, t, -jnp.inf)
        c_q = crow_ref[0, j, :, cs] * LOG2E
        m_old = m_scr[j, :, cs]
        m_new = jnp.maximum(m_old, c_q + jnp.max(t, axis=0, keepdims=True))
        alpha = jnp.exp2(m_old - m_new)
        p = jnp.exp2(t + (c_q - m_new))
        l_scr[j, :, cs] = alpha * l_scr[j, :, cs] + jnp.sum(p, axis=0, keepdims=True)
        pv = _dot(vt_ref[0, j, kt], p.astype(BF16))
        acc_scr[j, :, cs] = alpha * acc_scr[j, :, cs] + pv
        m_scr[j, :, cs] = m_new

    def run(blocks):
        ahead = 4
        ts = [scores(*b[:3]) for b in blocks[:ahead]]
        for n, (j, c, kt, diagonal) in enumerate(blocks):
            if n + ahead < len(blocks):
                ts.append(scores(*blocks[n + ahead][:3]))
            update(j, c, kt, diagonal, ts[n])

    def full_tile(kj, carry):
        run([(j, c, kj * nc + s, False)
             for s in range(nc) for j in range(nh) for c in range(nc)])
        return carry

    lax.fori_loop(0, qi, full_tile, 0)
    run([(j, c, qi * nc + s, c == s)
         for s in range(nc) for j in range(nh) for c in range(s, nc)])
    for j in range(nh):
        o = acc_scr[j] / l_scr[j]
        o_ref[0, :, j * C_HEAD_DIM:(j + 1) * C_HEAD_DIM] = o.T.astype(BF16)


def _fox_attention(qt, k, vt, c, *, tq=512, nh=4):
    bsz = k.shape[0]
    nc = tq // QC
    c_row = jnp.transpose(c[:, :, :C_HEADS], (0, 2, 1)).reshape(bsz, C_HEADS, 1, SEQ)
    return pl.pallas_call(
        functools.partial(_fox_attention_kernel, tq=tq, nh=nh),
        grid=(bsz, C_HEADS // nh, SEQ // tq),
        in_specs=[
            pl.BlockSpec((1, nh, nc, C_HEAD_DIM, QC), lambda b, h, i: (b, h, i, 0, 0)),
            pl.BlockSpec((1, nh, SEQ, C_HEAD_DIM), lambda b, h, i: (b, h, 0, 0)),
            pl.BlockSpec((1, nh, SEQ // QC, C_HEAD_DIM, QC), lambda b, h, i: (b, h, 0, 0, 0)),
            pl.BlockSpec((1, SEQ, LANES), lambda b, h, i: (b, 0, 0)),
            pl.BlockSpec((1, nh, 1, tq), lambda b, h, i: (b, h, 0, i)),
        ],
        out_specs=pl.BlockSpec((1, tq, nh * C_HEAD_DIM), lambda b, h, i: (b, i, h)),
        out_shape=jax.ShapeDtypeStruct((bsz, SEQ, C_WIDTH), BF16),
        scratch_shapes=[
            pltpu.VMEM((nh, SEQ, QC), F32),
            pltpu.VMEM((nh, 1, tq), F32),
            pltpu.VMEM((nh, 1, tq), F32),
            pltpu.VMEM((nh, C_HEAD_DIM, tq), F32),
        ],
        compiler_params=_params(3),
        name="fox_attention",
    )(qt, k, vt, c, c_row)


def _proj_residual_kernel(x_ref, a_ref, w_ref, o_ref):
    o_ref[0] = x_ref[0] + _dot(a_ref[0], w_ref[...])


def _proj_residual(x, a, w, *, tm=512):
    bsz = x.shape[0]
    row_spec = pl.BlockSpec((1, tm, D_MODEL), lambda b, i: (b, i, 0))
    return pl.pallas_call(
        _proj_residual_kernel,
        grid=(bsz, SEQ // tm),
        in_specs=[row_spec, pl.BlockSpec((1, tm, a.shape[2]), lambda b, i: (b, i, 0)),
                  _resident(w.shape)],
        out_specs=row_spec,
        out_shape=jax.ShapeDtypeStruct(x.shape, F32),
        compiler_params=_params(2),
        name="proj_residual",
    )(x, a, w.astype(BF16))


def kernel(x, norm_mix_g, norm_ffn_g, final_norm_g, w_in_even, ln_v_g, ln_v_b, w_spatial,
           b_spatial, w_pool, pool_scale, w_out_even, w_in_odd, b_forget, w_out_odd, w_up,
           conv_w, conv_b, w_down):
    depth = norm_mix_g.shape[0]
    assert x.shape[1:] == (SEQ, D_MODEL)
    for i in range(depth):
        if i % 2 == 0:
            e = i // 2
            x = _even_mixer(x, norm_mix_g[i], w_in_even[e], ln_v_g[e], ln_v_b[e], w_spatial[e],
                            b_spatial[e], w_pool[e], pool_scale[e], w_out_even[e])
        else:
            o = i // 2
            qt, k, vt, c = _qkv_proj(x, norm_mix_g[i], w_in_odd[o], b_forget[o])
            att = _fox_attention(qt, k, vt, c)
            x = _proj_residual(x, att, w_out_odd[o])
        x = _conv_ffn(x, norm_ffn_g[i], w_up[i], conv_w[i], conv_b[i], w_down[i],
                      final_norm_g if i == depth - 1 else None)
    if depth == 0:
        raise NotImplementedError("depth 0 is not a configuration of this problem")
    return x
```

```python
import functools

import jax
import jax.numpy as jnp
from jax import lax
from jax.experimental import pallas as pl
from jax.experimental.pallas import tpu as pltpu

D_MODEL = 2048
SEQ = 2048
EPS = 1e-6
CHUNK = 64
A_BLOCK = 128
A_GROUPS = 4
A_WIDTH = D_MODEL // 2
A_GROUP_DIM = A_WIDTH // A_GROUPS
B_WINDOWS = (2, 4, 8, 16)
B_WIDTH = D_MODEL // 2
B_GROUP_DIM = B_WIDTH // len(B_WINDOWS)
C_HEADS = 16
C_HEAD_DIM = 128
C_WIDTH = C_HEADS * C_HEAD_DIM
D_FF = 5632
CONV_WIDTH = 3

LANES = 128
SUBLANES = 8
VMEM_LIMIT_BYTES = 60 * 1024 * 1024

POOL_HALO = 16
CONV_HALO = SUBLANES
FFN_ROWS = 256
FFN_COLS = 256
QC = 256

BF16 = jnp.bfloat16
F32 = jnp.float32
LOG2E = 1.4426950408889634


def _dot(a, b):
    return jnp.dot(a, b, preferred_element_type=F32)


def _rms_norm(x, g):
    return x * lax.rsqrt(jnp.mean(x * x, axis=-1, keepdims=True) + EPS) * g


def _gelu_exact(x):
    return 0.5 * x * (1.0 + lax.erf(x * (0.5 ** 0.5)))


def _resident(shape):
    zeros = (0,) * len(shape)
    return pl.BlockSpec(shape, lambda *_: zeros, pipeline_mode=pl.Buffered(1))


def _params(n_grid):
    return pltpu.CompilerParams(
        dimension_semantics=("arbitrary",) * n_grid,
        vmem_limit_bytes=VMEM_LIMIT_BYTES,
    )


def _even_mixer_kernel(x_ref, g_ref, win_ref, lng_ref, lnb_ref, ws_ref, bs_ref,
                       wpool_ref, pscale_ref, wout_ref, o_ref, pool_ext, mix_scr, *, tm):
    si = pl.program_id(1)
    x = x_ref[0]
    h = _rms_norm(x, g_ref[...]).astype(BF16)

    u = _gelu_exact(_dot(h, win_ref[:, 0:A_WIDTH]))
    v = _gelu_exact(_dot(h, win_ref[:, A_WIDTH:2 * A_WIDTH]))
    t_idx = lax.broadcasted_iota(jnp.int32, (A_BLOCK, A_BLOCK), 0) // CHUNK
    s_idx = lax.broadcasted_iota(jnp.int32, (A_BLOCK, A_BLOCK), 1) // CHUNK
    causal = t_idx >= s_idx
    for g in range(A_GROUPS):
        cols = slice(g * A_GROUP_DIM, (g + 1) * A_GROUP_DIM)
        vg = v[:, cols]
        mu = jnp.mean(vg, axis=-1, keepdims=True)
        var = jnp.mean(jnp.square(vg - mu), axis=-1, keepdims=True)
        vn = ((vg - mu) * lax.rsqrt(var + EPS) * lng_ref[:, cols] + lnb_ref[:, cols]).astype(BF16)
        w_g = jnp.where(causal, ws_ref[g], 0.0).astype(BF16)
        for n in range(tm // A_BLOCK):
            rows = slice(n * A_BLOCK, (n + 1) * A_BLOCK)
            gate = _dot(w_g, vn[rows]) + bs_ref[g]
            mix_scr[rows, cols] = (u[rows, cols] * gate).astype(BF16)

    @pl.when(si == 0)
    def _():
        pool_ext[0:POOL_HALO, :] = jnp.zeros((POOL_HALO, B_WIDTH), F32)

    @pl.when(si > 0)
    def _():
        pool_ext[0:POOL_HALO, :] = pool_ext[tm:tm + POOL_HALO, :]

    pool_ext[POOL_HALO:POOL_HALO + tm, :] = _dot(h, win_ref[:, 2 * A_WIDTH:2 * A_WIDTH + B_WIDTH])
    pos = si * tm + lax.broadcasted_iota(jnp.int32, (tm, 1), 0)
    for j, w in enumerate(B_WINDOWS):
        cols = slice(j * B_GROUP_DIM, (j + 1) * B_GROUP_DIM)
        xg = pool_ext[POOL_HALO:POOL_HALO + tm, cols]
        acc = xg
        for k in range(1, w):
            acc = acc + pool_ext[POOL_HALO - k:POOL_HALO - k + tm, cols]
        count = jnp.minimum(pos + 1, w).astype(F32)
        d = (acc / count - xg).astype(BF16)
        yb = _dot(d, wpool_ref[j]) * pscale_ref[:, cols]
        mix_scr[:, A_WIDTH + j * B_GROUP_DIM:A_WIDTH + (j + 1) * B_GROUP_DIM] = yb.astype(BF16)

    o_ref[0] = x + _dot(mix_scr[...], wout_ref[...])


def _even_mixer(x, g, w_in, ln_g, ln_b, w_s, b_s, w_pool, p_scale, w_out, *, tm=256):
    bsz = x.shape[0]
    even_in = 2 * A_WIDTH + B_WIDTH
    row_spec = pl.BlockSpec((1, tm, D_MODEL), lambda b, i: (b, i, 0))
    return pl.pallas_call(
        functools.partial(_even_mixer_kernel, tm=tm),
        grid=(bsz, SEQ // tm),
        in_specs=[
            row_spec,
            _resident((1, D_MODEL)),
            _resident((D_MODEL, even_in)),
            _resident((1, A_WIDTH)),
            _resident((1, A_WIDTH)),
            _resident((A_GROUPS, A_BLOCK, A_BLOCK)),
            _resident((A_GROUPS, A_BLOCK, 1)),
            _resident((len(B_WINDOWS), B_GROUP_DIM, B_GROUP_DIM)),
            _resident((1, B_WIDTH)),
            _resident((A_WIDTH + B_WIDTH, D_MODEL)),
        ],
        out_specs=row_spec,
        out_shape=jax.ShapeDtypeStruct(x.shape, F32),
        scratch_shapes=[
            pltpu.VMEM((tm + POOL_HALO, B_WIDTH), F32),
            pltpu.VMEM((tm, A_WIDTH + B_WIDTH), BF16),
        ],
        compiler_params=_params(2),
        name="even_mixer",
    )(x, g.reshape(1, D_MODEL), w_in.astype(BF16), ln_g.reshape(1, A_WIDTH),
      ln_b.reshape(1, A_WIDTH), w_s, b_s.reshape(A_GROUPS, A_BLOCK, 1),
      w_pool.astype(BF16), p_scale.reshape(1, B_WIDTH), w_out.astype(BF16))


def _conv_ffn_kernel(x_ref, g_ref, wg_ref, wv_ref, cwg_ref, cwv_ref, cbg_ref, cbv_ref,
                     wd_ref, *rest, tm, nf, final_norm):
    if final_norm:
        gfin_ref, o_ref, h_scr, ext, act_scr, carry = rest
    else:
        o_ref, h_scr, ext, act_scr, carry = rest
    b = pl.program_id(0)
    si = pl.program_id(1)
    f = pl.program_id(2)

    row_blocks = [slice(r, r + FFN_ROWS) for r in range(0, tm, FFN_ROWS)]

    tf = wg_ref.shape[1]
    n_pieces = 2 * tf // FFN_COLS
    act_cols = tf // n_pieces

    def up_piece(rows, q):
        part, c0 = divmod(q * FFN_COLS, tf)
        cols = slice(c0, c0 + FFN_COLS)
        w_ref = (wg_ref, wv_ref)[part]
        up = _dot(h_scr[rows, :], w_ref[:, cols])
        if rows.start == 0:
            ext[part, 0:CONV_HALO, cols] = jnp.where(si > 0, carry[f, part, :, cols], 0.0)
        ext[part, CONV_HALO + rows.start:CONV_HALO + rows.stop, cols] = up
        if rows.stop == tm:
            carry[f, part, :, cols] = up[FFN_ROWS - CONV_HALO:FFN_ROWS, :]

    def act_piece(rows, q):
        cols = slice(q * act_cols, (q + 1) * act_cols)

        def causal_conv(part, cw_ref, cb_ref):
            lo = rows.start
            e = ext[part, lo:lo + CONV_HALO + FFN_ROWS, cols]
            out = cb_ref[:, cols]
            for j in range(CONV_WIDTH):
                lag = CONV_WIDTH - 1 - j
                lagged = pltpu.roll(e, lag, axis=0) if lag else e
                out = out + lagged[CONV_HALO:, :] * cw_ref[j:j + 1, cols]
            return out

        cg = causal_conv(0, cwg_ref, cbg_ref)
        cv = causal_conv(1, cwv_ref, cbv_ref)
        return (cg * (1.0 / (1.0 + jnp.exp(-cg))) * cv).astype(BF16)

    @pl.when(f == 0)
    def _():
        @pl.when((b == 0) & (si == 0))
        def _():
            carry[...] = jnp.zeros(carry.shape, F32)

        for rows in row_blocks:
            x = x_ref[0, rows, :]
            h_scr[rows, :] = _rms_norm(x, g_ref[...]).astype(BF16)
            o_ref[0, rows, :] = x

    for rows in row_blocks:
        for q in range(n_pieces):
            up_piece(rows, q)
    for rows in row_blocks:
        for q in range(n_pieces):
            act_scr[rows, q * act_cols:(q + 1) * act_cols] = act_piece(rows, q)
        o_ref[0, rows, :] += _dot(act_scr[rows, :], wd_ref[...])

    if final_norm:
        @pl.when(f == nf - 1)
        def _():
            for rows in row_blocks:
                o_ref[0, rows, :] = _rms_norm(o_ref[0, rows, :], gfin_ref[...])


def _conv_ffn(x, g, w_up, conv_w, conv_b, w_down, g_final=None, *, tm=1024, tf=512):
    bsz = x.shape[0]
    nf = D_FF // tf
    final_norm = g_final is not None
    row_spec = pl.BlockSpec((1, tm, D_MODEL), lambda b, i, f: (b, i, 0))
    w_up = w_up.astype(BF16)
    conv_b = conv_b.reshape(1, 2 * D_FF)

    in_specs = [
        row_spec,
        pl.BlockSpec((1, D_MODEL), lambda b, i, f: (0, 0)),
        pl.BlockSpec((D_MODEL, tf), lambda b, i, f: (0, f)),
        pl.BlockSpec((D_MODEL, tf), lambda b, i, f: (0, nf + f)),
        pl.BlockSpec((CONV_WIDTH, tf), lambda b, i, f: (0, f)),
        pl.BlockSpec((CONV_WIDTH, tf), lambda b, i, f: (0, nf + f)),
        pl.BlockSpec((1, tf), lambda b, i, f: (0, f)),
        pl.BlockSpec((1, tf), lambda b, i, f: (0, nf + f)),
        pl.BlockSpec((tf, D_MODEL), lambda b, i, f: (f, 0)),
    ]
    args = [x, g.reshape(1, D_MODEL), w_up, w_up, conv_w, conv_w, conv_b, conv_b,
            w_down.astype(BF16)]
    if final_norm:
        in_specs.append(pl.BlockSpec((1, D_MODEL), lambda b, i, f: (0, 0)))
        args.append(g_final.reshape(1, D_MODEL))
    return pl.pallas_call(
        functools.partial(_conv_ffn_kernel, tm=tm, nf=nf, final_norm=final_norm),
        grid=(bsz, SEQ // tm, nf),
        in_specs=in_specs,
        out_specs=row_spec,
        out_shape=jax.ShapeDtypeStruct(x.shape, F32),
        scratch_shapes=[
            pltpu.VMEM((tm, D_MODEL), BF16),
            pltpu.VMEM((2, tm + CONV_HALO, tf), F32),
            pltpu.VMEM((tm, tf), BF16),
            pltpu.VMEM((nf, 2, CONV_HALO, tf), F32),
        ],
        compiler_params=_params(3),
        name="conv_ffn_final" if final_norm else "conv_ffn",
    )(*args)


def _log_sigmoid(x):
    return jnp.minimum(x, 0.0) - jnp.log1p(jnp.exp(-jnp.abs(x)))


def _qkv_kernel(x_ref, g_ref, wqt_ref, wk_ref, wvt_ref, wf_ref, bf_ref,
                qt_ref, k_ref, vt_ref, c_ref, carry, *, tm):
    si = pl.program_id(1)
    hf = _rms_norm(x_ref[0], g_ref[...])
    h = hf.astype(BF16)
    ht = hf.T.astype(BF16)
    q_scale = C_HEAD_DIM ** -0.5 * LOG2E
    kk = _dot(h, wk_ref[...])
    qt = _dot(wqt_ref[...], ht) * q_scale
    vt = _dot(wvt_ref[...], ht)
    for hd in range(C_HEADS):
        dims = slice(hd * C_HEAD_DIM, (hd + 1) * C_HEAD_DIM)
        k_ref[0, hd] = kk[:, dims].astype(BF16)
        qt_ref[0, hd, 0] = qt[dims, :].astype(BF16)
        vt_ref[0, hd, 0] = vt[dims, :].astype(BF16)

    log_f = _log_sigmoid(_dot(h, wf_ref[...]) + bf_ref[...])
    row = lax.broadcasted_iota(jnp.int32, (tm, LANES), 0)
    k = 1
    while k < tm:
        log_f = log_f + jnp.where(row >= k, pltpu.roll(log_f, k, axis=0), 0.0)
        k *= 2

    @pl.when(si == 0)
    def _():
        carry[...] = jnp.zeros(carry.shape, F32)

    c = log_f + carry[0:1, :]
    c_ref[0] = c
    carry[...] = jnp.broadcast_to(c[tm - 1:tm, :], carry.shape)


def _qkv_proj(x, g, w_in, b_forget):
    bsz = x.shape[0]
    tm = QC
    wq_t = w_in[:, :C_WIDTH].T.astype(BF16)
    wk = w_in[:, C_WIDTH:2 * C_WIDTH].astype(BF16)
    wv_t = w_in[:, 2 * C_WIDTH:3 * C_WIDTH].T.astype(BF16)
    w_f = jnp.pad(w_in[:, 3 * C_WIDTH:], ((0, 0), (0, LANES - C_HEADS))).astype(BF16)
    b_f = jnp.pad(b_forget, (0, LANES - C_HEADS)).reshape(1, LANES)
    t_spec = pl.BlockSpec((1, C_HEADS, 1, C_HEAD_DIM, QC), lambda b, i: (b, 0, i, 0, 0))
    t_shape = jax.ShapeDtypeStruct((bsz, C_HEADS, SEQ // QC, C_HEAD_DIM, QC), BF16)
    return pl.pallas_call(
        functools.partial(_qkv_kernel, tm=tm),
        grid=(bsz, SEQ // tm),
        in_specs=[
            pl.BlockSpec((1, tm, D_MODEL), lambda b, i: (b, i, 0)),
            _resident((1, D_MODEL)),
            _resident((C_WIDTH, D_MODEL)),
            _resident((D_MODEL, C_WIDTH)),
            _resident((C_WIDTH, D_MODEL)),
            _resident((D_MODEL, LANES)),
            _resident((1, LANES)),
        ],
        out_specs=[t_spec,
                   pl.BlockSpec((1, C_HEADS, tm, C_HEAD_DIM), lambda b, i: (b, 0, i, 0)),
                   t_spec,
                   pl.BlockSpec((1, tm, LANES), lambda b, i: (b, i, 0))],
        out_shape=[t_shape,
                   jax.ShapeDtypeStruct((bsz, C_HEADS, SEQ, C_HEAD_DIM), BF16),
                   t_shape,
                   jax.ShapeDtypeStruct((bsz, SEQ, LANES), F32)],
        scratch_shapes=[pltpu.VMEM((SUBLANES, LANES), F32)],
        compiler_params=_params(2),
        name="qkv_proj",
    )(x, g.reshape(1, D_MODEL), wq_t, wk, wv_t, w_f, b_f)


def _fox_attention_kernel(qt_ref, k_ref, vt_ref, c_ref, crow_ref, o_ref,
                          ck_scr, m_scr, l_scr, acc_scr, *, tq, nh):
    hp = pl.program_id(1)
    qi = pl.program_id(2)
    nc = tq // QC

    @pl.when(qi == 0)
    def _():
        lane = lax.broadcasted_iota(jnp.int32, (SEQ, LANES), 1)
        c_all = c_ref[0] * LOG2E
        for j in range(nh):
            col = jnp.sum(jnp.where(lane == hp * nh + j, c_all, 0.0), axis=1, keepdims=True)
            ck_scr[j] = jnp.broadcast_to(col, (SEQ, QC))

    m_scr[...] = jnp.full(m_scr.shape, -jnp.inf, F32)
    l_scr[...] = jnp.zeros(l_scr.shape, F32)
    acc_scr[...] = jnp.zeros(acc_scr.shape, F32)

    def scores(j, c, kt):
        ks = pl.multiple_of(kt * QC, QC)
        return _dot(k_ref[0, j, pl.ds(ks, QC), :], qt_ref[0, j, c])

    def update(j, c, kt, diagonal, t):
        cs = slice(c * QC, (c + 1) * QC)
        ks = pl.multiple_of(kt * QC, QC)
        t = t - ck_scr[j, pl.ds(ks, QC), :]
        if diagonal:
            key = lax.broadcasted_iota(jnp.int32, (QC, QC), 0)
            qry = lax.broadcasted_iota(jnp.int32, (QC, QC), 1)
            t = jnp.where(key <= qry, t, -jnp.inf)
        c_q = crow_ref[0, j, :, cs] * LOG2E
        m_old = m_scr[j, :, cs]
        m_new = jnp.maximum(m_old, c_q + jnp.max(t, axis=0, keepdims=True))
        alpha = jnp.exp2(m_old - m_new)
        p = jnp.exp2(t + (c_q - m_new))
        l_scr[j, :, cs] = alpha * l_scr[j, :, cs] + jnp.sum(p, axis=0, keepdims=True)
        pv = _dot(vt_ref[0, j, kt], p.astype(BF16))
        acc_scr[j, :, cs] = alpha * acc_scr[j, :, cs] + pv
        m_scr[j, :, cs] = m_new

    def run(blocks):
        ahead = 4
        ts = [scores(*b[:3]) for b in blocks[:ahead]]
        for n, (j, c, kt, diagonal) in enumerate(blocks):
            if n + ahead < len(blocks):
                ts.append(scores(*blocks[n + ahead][:3]))
            update(j, c, kt, diagonal, ts[n])

    def full_tile(kj, carry):
        run([(j, c, kj * nc + s, False)
             for s in range(nc) for j in range(nh) for c in range(nc)])
        return carry

    lax.fori_loop(0, qi, full_tile, 0)
    run([(j, c, qi * nc + s, c == s)
         for s in range(nc) for j in range(nh) for c in range(s, nc)])
    for j in range(nh):
        o = acc_scr[j] / l_scr[j]
        o_ref[0, :, j * C_HEAD_DIM:(j + 1) * C_HEAD_DIM] = o.T.astype(BF16)


def _fox_attention(qt, k, vt, c, *, tq=512, nh=4):
    bsz = k.shape[0]
    nc = tq // QC
    c_row = jnp.transpose(c[:, :, :C_HEADS], (0, 2, 1)).reshape(bsz, C_HEADS, 1, SEQ)
    return pl.pallas_call(
        functools.partial(_fox_attention_kernel, tq=tq, nh=nh),
        grid=(bsz, C_HEADS // nh, SEQ // tq),
        in_specs=[
            pl.BlockSpec((1, nh, nc, C_HEAD_DIM, QC), lambda b, h, i: (b, h, i, 0, 0)),
            pl.BlockSpec((1, nh, SEQ, C_HEAD_DIM), lambda b, h, i: (b, h, 0, 0)),
            pl.BlockSpec((1, nh, SEQ // QC, C_HEAD_DIM, QC), lambda b, h, i: (b, h, 0, 0, 0)),
            pl.BlockSpec((1, SEQ, LANES), lambda b, h, i: (b, 0, 0)),
            pl.BlockSpec((1, nh, 1, tq), lambda b, h, i: (b, h, 0, i)),
        ],
        out_specs=pl.BlockSpec((1, tq, nh * C_HEAD_DIM), lambda b, h, i: (b, i, h)),
        out_shape=jax.ShapeDtypeStruct((bsz, SEQ, C_WIDTH), BF16),
        scratch_shapes=[
            pltpu.VMEM((nh, SEQ, QC), F32),
            pltpu.VMEM((nh, 1, tq), F32),
            pltpu.VMEM((nh, 1, tq), F32),
            pltpu.VMEM((nh, C_HEAD_DIM, tq), F32),
        ],
        compiler_params=_params(3),
        name="fox_attention",
    )(qt, k, vt, c, c_row)


def _proj_residual_kernel(x_ref, a_ref, w_ref, o_ref):
    o_ref[0] = x_ref[0] + _dot(a_ref[0], w_ref[...])


def _proj_residual(x, a, w, *, tm=512):
    bsz = x.shape[0]
    row_spec = pl.BlockSpec((1, tm, D_MODEL), lambda b, i: (b, i, 0))
    return pl.pallas_call(
        _proj_residual_kernel,
        grid=(bsz, SEQ // tm),
        in_specs=[row_spec, pl.BlockSpec((1, tm, a.shape[2]), lambda b, i: (b, i, 0)),
                  _resident(w.shape)],
        out_specs=row_spec,
        out_shape=jax.ShapeDtypeStruct(x.shape, F32),
        compiler_params=_params(2),
        name="proj_residual",
    )(x, a, w.astype(BF16))


def kernel(x, norm_mix_g, norm_ffn_g, final_norm_g, w_in_even, ln_v_g, ln_v_b, w_spatial,
           b_spatial, w_pool, pool_scale, w_out_even, w_in_odd, b_forget, w_out_odd, w_up,
           conv_w, conv_b, w_down):
    depth = norm_mix_g.shape[0]
    assert x.shape[1:] == (SEQ, D_MODEL)
    for i in range(depth):
        if i % 2 == 0:
            e = i // 2
            x = _even_mixer(x, norm_mix_g[i], w_in_even[e], ln_v_g[e], ln_v_b[e], w_spatial[e],
                            b_spatial[e], w_pool[e], pool_scale[e], w_out_even[e])
        else:
            o = i // 2
            qt, k, vt, c = _qkv_proj(x, norm_mix_g[i], w_in_odd[o], b_forget[o])
            att = _fox_attention(qt, k, vt, c)
            x = _proj_residual(x, att, w_out_odd[o])
        x = _conv_ffn(x, norm_ffn_g[i], w_up[i], conv_w[i], conv_b[i], w_down[i],
                      final_norm_g if i == depth - 1 else None)
    if depth == 0:
        raise NotImplementedError("depth 0 is not a configuration of this problem")
    return x
```

```python
import functools

import jax
import jax.numpy as jnp
from jax import lax
from jax.experimental import pallas as pl
from jax.experimental.pallas import tpu as pltpu

D_MODEL = 2048
SEQ = 2048
EPS = 1e-6
CHUNK = 64
A_BLOCK = 128
A_GROUPS = 4
A_WIDTH = D_MODEL // 2
A_GROUP_DIM = A_WIDTH // A_GROUPS
B_WINDOWS = (2, 4, 8, 16)
B_WIDTH = D_MODEL // 2
B_GROUP_DIM = B_WIDTH // len(B_WINDOWS)
C_HEADS = 16
C_HEAD_DIM = 128
C_WIDTH = C_HEADS * C_HEAD_DIM
D_FF = 5632
CONV_WIDTH = 3

LANES = 128
SUBLANES = 8
VMEM_LIMIT_BYTES = 60 * 1024 * 1024

POOL_HALO = 16
CONV_HALO = SUBLANES
FFN_ROWS = 256
FFN_COLS = 256
QC = 256

BF16 = jnp.bfloat16
F32 = jnp.float32
LOG2E = 1.4426950408889634


def _dot(a, b):
    return jnp.dot(a, b, preferred_element_type=F32)


def _rms_norm(x, g):
    return x * lax.rsqrt(jnp.mean(x * x, axis=-1, keepdims=True) + EPS) * g


def _gelu_exact(x):
    return 0.5 * x * (1.0 + lax.erf(x * (0.5 ** 0.5)))


def _resident(shape):
    zeros = (0,) * len(shape)
    return pl.BlockSpec(shape, lambda *_: zeros, pipeline_mode=pl.Buffered(1))


def _params(n_grid):
    return pltpu.CompilerParams(
        dimension_semantics=("arbitrary",) * n_grid,
        vmem_limit_bytes=VMEM_LIMIT_BYTES,
    )


def _even_mixer_kernel(x_ref, g_ref, win_ref, lng_ref, lnb_ref, ws_ref, bs_ref,
                       wpool_ref, pscale_ref, wout_ref, o_ref, pool_ext, mix_scr, *, tm):
    si = pl.program_id(1)
    x = x_ref[0]
    h = _rms_norm(x, g_ref[...]).astype(BF16)

    u = _gelu_exact(_dot(h, win_ref[:, 0:A_WIDTH]))
    v = _gelu_exact(_dot(h, win_ref[:, A_WIDTH:2 * A_WIDTH]))
    t_idx = lax.broadcasted_iota(jnp.int32, (A_BLOCK, A_BLOCK), 0) // CHUNK
    s_idx = lax.broadcasted_iota(jnp.int32, (A_BLOCK, A_BLOCK), 1) // CHUNK
    causal = t_idx >= s_idx
    for g in range(A_GROUPS):
        cols = slice(g * A_GROUP_DIM, (g + 1) * A_GROUP_DIM)
        vg = v[:, cols]
        mu = jnp.mean(vg, axis=-1, keepdims=True)
        var = jnp.mean(jnp.square(vg - mu), axis=-1, keepdims=True)
        vn = ((vg - mu) * lax.rsqrt(var + EPS) * lng_ref[:, cols] + lnb_ref[:, cols]).astype(BF16)
        w_g = jnp.where(causal, ws_ref[g], 0.0).astype(BF16)
        for n in range(tm // A_BLOCK):
            rows = slice(n * A_BLOCK, (n + 1) * A_BLOCK)
            gate = _dot(w_g, vn[rows]) + bs_ref[g]
            mix_scr[rows, cols] = (u[rows, cols] * gate).astype(BF16)

    @pl.when(si == 0)
    def _():
        pool_ext[0:POOL_HALO, :] = jnp.zeros((POOL_HALO, B_WIDTH), F32)

    @pl.when(si > 0)
    def _():
        pool_ext[0:POOL_HALO, :] = pool_ext[tm:tm + POOL_HALO, :]

    pool_ext[POOL_HALO:POOL_HALO + tm, :] = _dot(h, win_ref[:, 2 * A_WIDTH:2 * A_WIDTH + B_WIDTH])
    pos = si * tm + lax.broadcasted_iota(jnp.int32, (tm, 1), 0)
    for j, w in enumerate(B_WINDOWS):
        cols = slice(j * B_GROUP_DIM, (j + 1) * B_GROUP_DIM)
        xg = pool_ext[POOL_HALO:POOL_HALO + tm, cols]
        acc = xg
        for k in range(1, w):
            acc = acc + pool_ext[POOL_HALO - k:POOL_HALO - k + tm, cols]
        count = jnp.minimum(pos + 1, w).astype(F32)
        d = (acc / count - xg).astype(BF16)
        yb = _dot(d, wpool_ref[j]) * pscale_ref[:, cols]
        mix_scr[:, A_WIDTH + j * B_GROUP_DIM:A_WIDTH + (j + 1) * B_GROUP_DIM] = yb.astype(BF16)

    o_ref[0] = x + _dot(mix_scr[...], wout_ref[...])


def _even_mixer(x, g, w_in, ln_g, ln_b, w_s, b_s, w_pool, p_scale, w_out, *, tm=512):
    bsz = x.shape[0]
    even_in = 2 * A_WIDTH + B_WIDTH
    row_spec = pl.BlockSpec((1, tm, D_MODEL), lambda b, i: (b, i, 0))
    return pl.pallas_call(
        functools.partial(_even_mixer_kernel, tm=tm),
        grid=(bsz, SEQ // tm),
        in_specs=[
            row_spec,
            _resident((1, D_MODEL)),
            _resident((D_MODEL, even_in)),
            _resident((1, A_WIDTH)),
            _resident((1, A_WIDTH)),
            _resident((A_GROUPS, A_BLOCK, A_BLOCK)),
            _resident((A_GROUPS, A_BLOCK, 1)),
            _resident((len(B_WINDOWS), B_GROUP_DIM, B_GROUP_DIM)),
            _resident((1, B_WIDTH)),
            _resident((A_WIDTH + B_WIDTH, D_MODEL)),
        ],
        out_specs=row_spec,
        out_shape=jax.ShapeDtypeStruct(x.shape, F32),
        scratch_shapes=[
            pltpu.VMEM((tm + POOL_HALO, B_WIDTH), F32),
            pltpu.VMEM((tm, A_WIDTH + B_WIDTH), BF16),
        ],
        compiler_params=_params(2),
        name="even_mixer",
    )(x, g.reshape(1, D_MODEL), w_in.astype(BF16), ln_g.reshape(1, A_WIDTH),
      ln_b.reshape(1, A_WIDTH), w_s, b_s.reshape(A_GROUPS, A_BLOCK, 1),
      w_pool.astype(BF16), p_scale.reshape(1, B_WIDTH), w_out.astype(BF16))


def _conv_ffn_kernel(x_ref, g_ref, wg_ref, wv_ref, cwg_ref, cwv_ref, cbg_ref, cbv_ref,
                     wd_ref, *rest, tm, nf, final_norm):
    if final_norm:
        gfin_ref, o_ref, h_scr, ext, act_scr, carry = rest
    else:
        o_ref, h_scr, ext, act_scr, carry = rest
    b = pl.program_id(0)
    si = pl.program_id(1)
    f = pl.program_id(2)

    row_blocks = [slice(r, r + FFN_ROWS) for r in range(0, tm, FFN_ROWS)]

    tf = wg_ref.shape[1]
    n_pieces = 2 * tf // FFN_COLS
    act_cols = tf // n_pieces

    def up_piece(rows, q):
        part, c0 = divmod(q * FFN_COLS, tf)
        cols = slice(c0, c0 + FFN_COLS)
        w_ref = (wg_ref, wv_ref)[part]
        up = _dot(h_scr[rows, :], w_ref[:, cols])
        if rows.start == 0:
            ext[part, 0:CONV_HALO, cols] = jnp.where(si > 0, carry[f, part, :, cols], 0.0)
        ext[part, CONV_HALO + rows.start:CONV_HALO + rows.stop, cols] = up
        if rows.stop == tm:
            carry[f, part, :, cols] = up[FFN_ROWS - CONV_HALO:FFN_ROWS, :]

    def act_piece(rows, q):
        cols = slice(q * act_cols, (q + 1) * act_cols)

        def causal_conv(part, cw_ref, cb_ref):
            lo = rows.start
            e = ext[part, lo:lo + CONV_HALO + FFN_ROWS, cols]
            out = cb_ref[:, cols]
            for j in range(CONV_WIDTH):
                lag = CONV_WIDTH - 1 - j
                lagged = pltpu.roll(e, lag, axis=0) if lag else e
                out = out + lagged[CONV_HALO:, :] * cw_ref[j:j + 1, cols]
            return out

        cg = causal_conv(0, cwg_ref, cbg_ref)
        cv = causal_conv(1, cwv_ref, cbv_ref)
        return (cg * (1.0 / (1.0 + jnp.exp(-cg))) * cv).astype(BF16)

    @pl.when(f == 0)
    def _():
        @pl.when((b == 0) & (si == 0))
        def _():
            carry[...] = jnp.zeros(carry.shape, F32)

        for rows in row_blocks:
            x = x_ref[0, rows, :]
            h_scr[rows, :] = _rms_norm(x, g_ref[...]).astype(BF16)
            o_ref[0, rows, :] = x

    for rows in row_blocks:
        for q in range(n_pieces):
            up_piece(rows, q)
    for rows in row_blocks:
        for q in range(n_pieces):
            act_scr[rows, q * act_cols:(q + 1) * act_cols] = act_piece(rows, q)
        o_ref[0, rows, :] += _dot(act_scr[rows, :], wd_ref[...])

    if final_norm:
        @pl.when(f == nf - 1)
        def _():
            for rows in row_blocks:
                o_ref[0, rows, :] = _rms_norm(o_ref[0, rows, :], gfin_ref[...])


def _conv_ffn(x, g, w_up, conv_w, conv_b, w_down, g_final=None, *, tm=1024, tf=512):
    bsz = x.shape[0]
    nf = D_FF // tf
    final_norm = g_final is not None
    row_spec = pl.BlockSpec((1, tm, D_MODEL), lambda b, i, f: (b, i, 0))
    w_up = w_up.astype(BF16)
    conv_b = conv_b.reshape(1, 2 * D_FF)

    in_specs = [
        row_spec,
        pl.BlockSpec((1, D_MODEL), lambda b, i, f: (0, 0)),
        pl.BlockSpec((D_MODEL, tf), lambda b, i, f: (0, f)),
        pl.BlockSpec((D_MODEL, tf), lambda b, i, f: (0, nf + f)),
        pl.BlockSpec((CONV_WIDTH, tf), lambda b, i, f: (0, f)),
        pl.BlockSpec((CONV_WIDTH, tf), lambda b, i, f: (0, nf + f)),
        pl.BlockSpec((1, tf), lambda b, i, f: (0, f)),
        pl.BlockSpec((1, tf), lambda b, i, f: (0, nf + f)),
        pl.BlockSpec((tf, D_MODEL), lambda b, i, f: (f, 0)),
    ]
    args = [x, g.reshape(1, D_MODEL), w_up, w_up, conv_w, conv_w, conv_b, conv_b,
            w_down.astype(BF16)]
    if final_norm:
        in_specs.append(pl.BlockSpec((1, D_MODEL), lambda b, i, f: (0, 0)))
        args.append(g_final.reshape(1, D_MODEL))
    return pl.pallas_call(
        functools.partial(_conv_ffn_kernel, tm=tm, nf=nf, final_norm=final_norm),
        grid=(bsz, SEQ // tm, nf),
        in_specs=in_specs,
        out_specs=row_spec,
        out_shape=jax.ShapeDtypeStruct(x.shape, F32),
        scratch_shapes=[
            pltpu.VMEM((tm, D_MODEL), BF16),
            pltpu.VMEM((2, tm + CONV_HALO, tf), F32),
            pltpu.VMEM((tm, tf), BF16),
            pltpu.VMEM((nf, 2, CONV_HALO, tf), F32),
        ],
        compiler_params=_params(3),
        name="conv_ffn_final" if final_norm else "conv_ffn",
    )(*args)


def _log_sigmoid(x):
    return jnp.minimum(x, 0.0) - jnp.log1p(jnp.exp(-jnp.abs(x)))


def _qkv_kernel(x_ref, g_ref, wqt_ref, wk_ref, wvt_ref, wf_ref, bf_ref,
                qt_ref, k_ref, vt_ref, c_ref, carry, *, tm):
    si = pl.program_id(1)
    hf = _rms_norm(x_ref[0], g_ref[...])
    h = hf.astype(BF16)
    ht = hf.T.astype(BF16)
    q_scale = C_HEAD_DIM ** -0.5 * LOG2E
    kk = _dot(h, wk_ref[...])
    qt = _dot(wqt_ref[...], ht) * q_scale
    vt = _dot(wvt_ref[...], ht)
    for hd in range(C_HEADS):
        dims = slice(hd * C_HEAD_DIM, (hd + 1) * C_HEAD_DIM)
        k_ref[0, hd] = kk[:, dims].astype(BF16)
        qt_ref[0, hd, 0] = qt[dims, :].astype(BF16)
        vt_ref[0, hd, 0] = vt[dims, :].astype(BF16)

    log_f = _log_sigmoid(_dot(h, wf_ref[...]) + bf_ref[...])
    row = lax.broadcasted_iota(jnp.int32, (tm, LANES), 0)
    k = 1
    while k < tm:
        log_f = log_f + jnp.where(row >= k, pltpu.roll(log_f, k, axis=0), 0.0)
        k *= 2

    @pl.when(si == 0)
    def _():
        carry[...] = jnp.zeros(carry.shape, F32)

    c = log_f + carry[0:1, :]
    c_ref[0] = c
    carry[...] = jnp.broadcast_to(c[tm - 1:tm, :], carry.shape)


def _qkv_proj(x, g, w_in, b_forget):
    bsz = x.shape[0]
    tm = QC
    wq_t = w_in[:, :C_WIDTH].T.astype(BF16)
    wk = w_in[:, C_WIDTH:2 * C_WIDTH].astype(BF16)
    wv_t = w_in[:, 2 * C_WIDTH:3 * C_WIDTH].T.astype(BF16)
    w_f = jnp.pad(w_in[:, 3 * C_WIDTH:], ((0, 0), (0, LANES - C_HEADS))).astype(BF16)
    b_f = jnp.pad(b_forget, (0, LANES - C_HEADS)).reshape(1, LANES)
    t_spec = pl.BlockSpec((1, C_HEADS, 1, C_HEAD_DIM, QC), lambda b, i: (b, 0, i, 0, 0))
    t_shape = jax.ShapeDtypeStruct((bsz, C_HEADS, SEQ // QC, C_HEAD_DIM, QC), BF16)
    return pl.pallas_call(
        functools.partial(_qkv_kernel, tm=tm),
        grid=(bsz, SEQ // tm),
        in_specs=[
            pl.BlockSpec((1, tm, D_MODEL), lambda b, i: (b, i, 0)),
            _resident((1, D_MODEL)),
            _resident((C_WIDTH, D_MODEL)),
            _resident((D_MODEL, C_WIDTH)),
            _resident((C_WIDTH, D_MODEL)),
            _resident((D_MODEL, LANES)),
            _resident((1, LANES)),
        ],
        out_specs=[t_spec,
                   pl.BlockSpec((1, C_HEADS, tm, C_HEAD_DIM), lambda b, i: (b, 0, i, 0)),
                   t_spec,
                   pl.BlockSpec((1, tm, LANES), lambda b, i: (b, i, 0))],
        out_shape=[t_shape,
                   jax.ShapeDtypeStruct((bsz, C_HEADS, SEQ, C_HEAD_DIM), BF16),
                   t_shape,
                   jax.ShapeDtypeStruct((bsz, SEQ, LANES), F32)],
        scratch_shapes=[pltpu.VMEM((SUBLANES, LANES), F32)],
        compiler_params=_params(2),
        name="qkv_proj",
    )(x, g.reshape(1, D_MODEL), wq_t, wk, wv_t, w_f, b_f)


def _fox_attention_kernel(qt_ref, k_ref, vt_ref, c_ref, crow_ref, o_ref,
                          ck_scr, m_scr, l_scr, acc_scr, *, tq, nh):
    hp = pl.program_id(1)
    qi = pl.program_id(2)
    nc = tq // QC

    @pl.when(qi == 0)
    def _():
        lane = lax.broadcasted_iota(jnp.int32, (SEQ, LANES), 1)
        c_all = c_ref[0] * LOG2E
        for j in range(nh):
            col = jnp.sum(jnp.where(lane == hp * nh + j, c_all, 0.0), axis=1, keepdims=True)
            ck_scr[j] = jnp.broadcast_to(col, (SEQ, QC))

    m_scr[...] = jnp.full(m_scr.shape, -jnp.inf, F32)
    l_scr[...] = jnp.zeros(l_scr.shape, F32)
    acc_scr[...] = jnp.zeros(acc_scr.shape, F32)

    def scores(j, c, kt):
        ks = pl.multiple_of(kt * QC, QC)
        return _dot(k_ref[0, j, pl.ds(ks, QC), :], qt_ref[0, j, c])

    def update(j, c, kt, diagonal, t):
        cs = slice(c * QC, (c + 1) * QC)
        ks = pl.multiple_of(kt * QC, QC)
        t = t - ck_scr[j, pl.ds(ks, QC), :]
        if diagonal:
            key = lax.broadcasted_iota(jnp.int32, (QC, QC), 0)
            qry = lax.broadcasted_iota(jnp.int32, (QC, QC), 1)
            t = jnp.where(key <= qry, t, -jnp.inf)
        c_q = crow_ref[0, j, :, cs] * LOG2E
        m_old = m_scr[j, :, cs]
        m_new = jnp.maximum(m_old, c_q + jnp.max(t, axis=0, keepdims=True))
        alpha = jnp.exp2(m_old - m_new)
        p = jnp.exp2(t + (c_q - m_new))
        l_scr[j, :, cs] = alpha * l_scr[j, :, cs] + jnp.sum(p, axis=0, keepdims=True)
        pv = _dot(vt_ref[0, j, kt], p.astype(BF16))
        acc_scr[j, :, cs] = alpha * acc_scr[j, :, cs] + pv
        m_scr[j, :, cs] = m_new

    def run(blocks):
        ahead = 4
        ts = [scores(*b[:3]) for b in blocks[:ahead]]
        for n, (j, c, kt, diagonal) in enumerate(blocks):
            if n + ahead < len(blocks):
                ts.append(scores(*blocks[n + ahead][:3]))
            update(j, c, kt, diagonal, ts[n])

    def full_tile(kj, carry):
        run([(j, c, kj * nc + s, False)
             for s in range(nc) for j in range(nh) for c in range(nc)])
        return carry

    lax.fori_loop(0, qi, full_tile, 0)
    run([(j, c, qi * nc + s, c == s)
         for s in range(nc) for j in range(nh) for c in range(s, nc)])
    for j in range(nh):
        o = acc_scr[j] / l_scr[j]
        o_ref[0, :, j * C_HEAD_DIM:(j + 1) * C_HEAD_DIM] = o.T.astype(BF16)


def _fox_attention(qt, k, vt, c, *, tq=512, nh=4):
    bsz = k.shape[0]
    nc = tq // QC
    c_row = jnp.transpose(c[:, :, :C_HEADS], (0, 2, 1)).reshape(bsz, C_HEADS, 1, SEQ)
    return pl.pallas_call(
        functools.partial(_fox_attention_kernel, tq=tq, nh=nh),
        grid=(bsz, C_HEADS // nh, SEQ // tq),
        in_specs=[
            pl.BlockSpec((1, nh, nc, C_HEAD_DIM, QC), lambda b, h, i: (b, h, i, 0, 0)),
            pl.BlockSpec((1, nh, SEQ, C_HEAD_DIM), lambda b, h, i: (b, h, 0, 0)),
            pl.BlockSpec((1, nh, SEQ // QC, C_HEAD_DIM, QC), lambda b, h, i: (b, h, 0, 0, 0)),
            pl.BlockSpec((1, SEQ, LANES), lambda b, h, i: (b, 0, 0)),
            pl.BlockSpec((1, nh, 1, tq), lambda b, h, i: (b, h, 0, i)),
        ],
        out_specs=pl.BlockSpec((1, tq, nh * C_HEAD_DIM), lambda b, h, i: (b, i, h)),
        out_shape=jax.ShapeDtypeStruct((bsz, SEQ, C_WIDTH), BF16),
        scratch_shapes=[
            pltpu.VMEM((nh, SEQ, QC), F32),
            pltpu.VMEM((nh, 1, tq), F32),
            pltpu.VMEM((nh, 1, tq), F32),
            pltpu.VMEM((nh, C_HEAD_DIM, tq), F32),
        ],
        compiler_params=_params(3),
        name="fox_attention",
    )(qt, k, vt, c, c_row)


def _proj_residual_kernel(x_ref, a_ref, w_ref, o_ref):
    o_ref[0] = x_ref[0] + _dot(a_ref[0], w_ref[...])


def _proj_residual(x, a, w, *, tm=512):
    bsz = x.shape[0]
    row_spec = pl.BlockSpec((1, tm, D_MODEL), lambda b, i: (b, i, 0))
    return pl.pallas_call(
        _proj_residual_kernel,
        grid=(bsz, SEQ // tm),
        in_specs=[row_spec, pl.BlockSpec((1, tm, a.shape[2]), lambda b, i: (b, i, 0)),
                  _resident(w.shape)],
        out_specs=row_spec,
        out_shape=jax.ShapeDtypeStruct(x.shape, F32),
        compiler_params=_params(2),
        name="proj_residual",
    )(x, a, w.astype(BF16))


def kernel(x, norm_mix_g, norm_ffn_g, final_norm_g, w_in_even, ln_v_g, ln_v_b, w_spatial,
           b_spatial, w_pool, pool_scale, w_out_even, w_in_odd, b_forget, w_out_odd, w_up,
           conv_w, conv_b, w_down):
    depth = norm_mix_g.shape[0]
    assert x.shape[1:] == (SEQ, D_MODEL)
    for i in range(depth):
        if i % 2 == 0:
            e = i // 2
            x = _even_mixer(x, norm_mix_g[i], w_in_even[e], ln_v_g[e], ln_v_b[e], w_spatial[e],
                            b_spatial[e], w_pool[e], pool_scale[e], w_out_even[e])
        else:
            o = i // 2
            qt, k, vt, c = _qkv_proj(x, norm_mix_g[i], w_in_odd[o], b_forget[o])
            att = _fox_attention(qt, k, vt, c)
            x = _proj_residual(x, att, w_out_odd[o])
        x = _conv_ffn(x, norm_ffn_g[i], w_up[i], conv_w[i], conv_b[i], w_down[i],
                      final_norm_g if i == depth - 1 else None)
    if depth == 0:
        raise NotImplementedError("depth 0 is not a configuration of this problem")
    return x
```
